```python
import math
import jax, jax.numpy as jnp
from jax import lax
import numpy as np

D_MODEL = 1024
BATCH = 16
SEQ = 4096
DEPTH = 4

N_MIXERS = 4
D_FF = 4 * D_MODEL
EPS = 1e-6

SSM_GROUP = 16
SSM_GROUPS = D_MODEL // SSM_GROUP
SSM_STATE = 64
DT_MIN = 1e-3
DT_MAX = 1e-1

CONV_WIDTH = 31

GMLP_CHUNK = 128
GMLP_HEADS = 4
GMLP_WIDTH = D_MODEL

ATT_CONFIGS = ((128, 1), (512, 4), (2048, 16))
ATT_GROUPS = len(ATT_CONFIGS)
ATT_HEADS = 8
HEAD_DIM = 64
ATT_GROUP_WIDTH = ATT_HEADS * HEAD_DIM

kernel_name = "interleaved_s5_conv_gmlp_dilated_attn_trunk"


def _rmsnorm(x, g):
    xf = x.astype(jnp.float32)
    y = xf * lax.rsqrt(jnp.mean(xf * xf, axis=-1, keepdims=True) + EPS)
    return (y * g.astype(jnp.float32)).astype(x.dtype)


def _layernorm(x, g, b):
    xf = x.astype(jnp.float32)
    mu = jnp.mean(xf, axis=-1, keepdims=True)
    var = jnp.mean(jnp.square(xf - mu), axis=-1, keepdims=True)
    y = (xf - mu) * lax.rsqrt(var + EPS)
    return (y * g.astype(jnp.float32) + b.astype(jnp.float32)).astype(x.dtype)


def _s5_mixer(h, a_re, a_im, b_re, b_im, c_re, c_im, d_skip, log_dt, w_glu):
    bsz, s, _ = h.shape
    f32 = jnp.float32
    u = h.astype(f32).reshape(bsz, s, SSM_GROUPS, SSM_GROUP)
    a = lax.complex(a_re.astype(f32), a_im.astype(f32))
    dt = jnp.exp(log_dt.astype(f32))[:, None]
    a_bar = jnp.exp(a * dt)
    b_mat = lax.complex(b_re.astype(f32), b_im.astype(f32))
    b_bar = ((a_bar - 1.0) / a)[..., None] * b_mat
    bu = jnp.einsum('bsgp,gnp->sbgn', u.astype(jnp.complex64), b_bar)
    a_elems = jnp.broadcast_to(a_bar[None, None], (s, 1, SSM_GROUPS, SSM_STATE))

    def combine(left, right):
        a_l, b_l = left
        a_r, b_r = right
        return a_r * a_l, a_r * b_l + b_r

    _, states = lax.associative_scan(combine, (a_elems, bu), axis=0)
    c_mat = lax.complex(c_re.astype(f32), c_im.astype(f32))
    y = jnp.real(jnp.einsum('sbgn,gpn->bsgp', states, c_mat))
    y = y + d_skip.astype(f32).reshape(SSM_GROUPS, SSM_GROUP) * u
    y = jax.nn.gelu(y.reshape(bsz, s, D_MODEL)).astype(h.dtype)
    z = y @ w_glu
    return z[..., :D_MODEL] * jax.nn.sigmoid(z[..., D_MODEL:])


def _conv_mixer(h, w_pw1, b_pw1, w_dw, b_dw, ln_g, ln_b, w_pw2, b_pw2):
    z = h @ w_pw1 + b_pw1
    z = z[..., :D_MODEL] * jax.nn.sigmoid(z[..., D_MODEL:])
    y = lax.conv_general_dilated(
        z, w_dw[:, None, :].astype(z.dtype), window_strides=(1,),
        padding=((CONV_WIDTH - 1, 0),),
        dimension_numbers=('NWC', 'WIO', 'NWC'),
        feature_group_count=D_MODEL) + b_dw
    y = jax.nn.silu(_layernorm(y, ln_g, ln_b))
    return y @ w_pw2 + b_pw2


def _gmlp_mixer(h, w_in, ln_g, ln_b, w_s, b_s, w_out):
    bsz, s, _ = h.shape
    z = jax.nn.gelu(h @ w_in)
    u, v = z[..., :GMLP_WIDTH], z[..., GMLP_WIDTH:]
    v = _layernorm(v, ln_g, ln_b)
    n_chunks = s // GMLP_CHUNK
    v = v.reshape(bsz, n_chunks, GMLP_CHUNK, GMLP_HEADS, GMLP_WIDTH // GMLP_HEADS)
    causal = jnp.tril(jnp.ones((GMLP_CHUNK, GMLP_CHUNK), dtype=bool))
    ws = jnp.where(causal[None], w_s, 0.0)
    v = jnp.einsum('hts,bcshe->bcthe', ws, v) + b_s.T[None, None, :, :, None]
    v = v.reshape(bsz, s, GMLP_WIDTH)
    return (u * v) @ w_out


def _dilated_window_attention(q, k, v, window, dil):
    bsz, s, nh, hd = q.shape
    steps = window // dil
    blk = steps
    span = dil * blk
    s_pad = -(-s // span) * span
    pad = ((0, 0), (0, s_pad - s), (0, 0), (0, 0))
    nb = s_pad // span

    def split(t):
        t = jnp.pad(t, pad)
        return t.reshape(bsz, nb, blk, dil, nh, hd).transpose(0, 3, 1, 2, 4, 5)

    def with_prev(t):
        prev = jnp.pad(t, ((0, 0), (0, 0), (1, 0), (0, 0), (0, 0), (0, 0)))[:, :, :-1]
        return jnp.concatenate([prev, t], axis=3)

    qb = split(q)
    kk = with_prev(split(k))
    vv = with_prev(split(v))
    scores = jnp.einsum('brnihd,brnjhd->brnhij', qb, kk,
                        preferred_element_type=jnp.float32) * (HEAD_DIM ** -0.5)
    i_idx = jnp.arange(blk)[:, None]
    j_idx = jnp.arange(2 * blk)[None, :]
    dist = i_idx + blk - j_idx
    band = (dist >= 0) & (dist <= steps)
    has_prev = (jnp.arange(nb) > 0)[:, None, None]
    valid = band[None] & (has_prev | (j_idx >= blk)[None])
    scores = jnp.where(valid[None, None, :, None], scores, -jnp.inf)
    lse = jax.nn.logsumexp(scores, axis=-1)
    probs = jnp.exp(scores - lse[..., None])
    out = jnp.einsum('brnhij,brnjhd->brnihd', probs, vv.astype(jnp.float32))
    out = out.transpose(0, 2, 3, 1, 4, 5).reshape(bsz, s_pad, nh, hd)[:, :s]
    lse = lse.transpose(0, 2, 4, 1, 3).reshape(bsz, s_pad, nh)[:, :s]
    return out, lse


def _attention_mixer(h, w_qkv, w_o):
    bsz, s, _ = h.shape
    qkv = (h @ w_qkv).reshape(bsz, s, 3, ATT_GROUPS, ATT_HEADS, HEAD_DIM)
    outs, lses = [], []
    for g, (window, dil) in enumerate(ATT_CONFIGS):
        o, l = _dilated_window_attention(qkv[:, :, 0, g], qkv[:, :, 1, g], qkv[:, :, 2, g], window, dil)
        outs.append(o)
        lses.append(l)
    outs = jnp.stack(outs, axis=0)
    weights = jax.nn.softmax(jnp.stack(lses, axis=0), axis=0)
    merged = jnp.sum(weights[..., None] * outs, axis=0)
    return merged.reshape(bsz, s, ATT_GROUP_WIDTH).astype(h.dtype) @ w_o


def _mlp(h, w_in, w_out):
    return jnp.square(jax.nn.relu(h @ w_in)) @ w_out


def _n_layers_of(m):
    return len(range(m, DEPTH, N_MIXERS))


def setup_inputs(seed: int = 0) -> dict:
    key = jax.random.key(seed)
    ks = jax.random.split(key, 32)
    nrm = jax.random.normal
    f32 = jnp.float32
    D, G, N, P = D_MODEL, SSM_GROUPS, SSM_STATE, SSM_GROUP
    nA, nB, nC, nD = (_n_layers_of(m) for m in range(N_MIXERS))
    E, T = GMLP_WIDTH, GMLP_CHUNK
    qkv_width = 3 * ATT_GROUPS * ATT_GROUP_WIDTH
    n_idx = jnp.arange(N, dtype=f32)
    return {
        "x": nrm(ks[0], (BATCH, SEQ, D), f32),
        "norm_mix": 1.0 + 0.02 * nrm(ks[1], (DEPTH, D), f32),
        "norm_mlp": 1.0 + 0.02 * nrm(ks[2], (DEPTH, D), f32),
        "norm_final": 1.0 + 0.02 * nrm(ks[3], (D,), f32),
        "ssm_a_re": -0.5 + 0.01 * nrm(ks[4], (nA, G, N), f32),
        "ssm_a_im": math.pi * n_idx + 0.01 * nrm(ks[5], (nA, G, N), f32),
        "ssm_b_re": nrm(ks[6], (nA, G, N, P), f32) * (2 * P) ** -0.5,
        "ssm_b_im": nrm(ks[7], (nA, G, N, P), f32) * (2 * P) ** -0.5,
        "ssm_c_re": nrm(ks[8], (nA, G, P, N), f32) * (2 * N) ** -0.5,
        "ssm_c_im": nrm(ks[9], (nA, G, P, N), f32) * (2 * N) ** -0.5,
        "ssm_d": nrm(ks[10], (nA, D), f32),
        "ssm_log_dt": jax.random.uniform(ks[11], (nA, G), f32, math.log(DT_MIN), math.log(DT_MAX)),
        "ssm_w_glu": nrm(ks[12], (nA, D, 2 * D), f32) * D ** -0.5,
        "conv_w_pw1": nrm(ks[13], (nB, D, 2 * D), f32) * D ** -0.5,
        "conv_b_pw1": 0.01 * nrm(ks[14], (nB, 2 * D), f32),
        "conv_w_dw": nrm(ks[15], (nB, CONV_WIDTH, D), f32) * CONV_WIDTH ** -0.5,
        "conv_b_dw": 0.01 * nrm(ks[16], (nB, D), f32),
        "conv_ln_g": 1.0 + 0.02 * nrm(ks[17], (nB, D), f32),
        "conv_ln_b": 0.01 * nrm(ks[18], (nB, D), f32),
        "conv_w_pw2": nrm(ks[19], (nB, D, D), f32) * D ** -0.5,
        "conv_b_pw2": 0.01 * nrm(ks[20], (nB, D), f32),
        "gmlp_w_in": nrm(ks[21], (nC, D, 2 * E), f32) * D ** -0.5,
        "gmlp_ln_g": 1.0 + 0.02 * nrm(ks[22], (nC, E), f32),
        "gmlp_ln_b": 0.01 * nrm(ks[23], (nC, E), f32),
        "gmlp_w_s": nrm(ks[24], (nC, GMLP_HEADS, T, T), f32) * T ** -0.5,
        "gmlp_b_s": 1.0 + 0.02 * nrm(ks[25], (nC, GMLP_HEADS, T), f32),
        "gmlp_w_out": nrm(ks[26], (nC, E, D), f32) * E ** -0.5,
        "attn_w_qkv": nrm(ks[27], (nD, D, qkv_width), f32) * D ** -0.5,
        "attn_w_o": nrm(ks[28], (nD, ATT_GROUP_WIDTH, D), f32) * ATT_GROUP_WIDTH ** -0.5,
        "mlp_w_in": nrm(ks[29], (DEPTH, D, D_FF), f32) * D ** -0.5,
        "mlp_w_out": nrm(ks[30], (DEPTH, D_FF, D), f32) * D_FF ** -0.5,
    }


def reference(x, norm_mix, norm_mlp, norm_final,
              ssm_a_re, ssm_a_im, ssm_b_re, ssm_b_im, ssm_c_re, ssm_c_im, ssm_d, ssm_log_dt, ssm_w_glu,
              conv_w_pw1, conv_b_pw1, conv_w_dw, conv_b_dw, conv_ln_g, conv_ln_b, conv_w_pw2, conv_b_pw2,
              gmlp_w_in, gmlp_ln_g, gmlp_ln_b, gmlp_w_s, gmlp_b_s, gmlp_w_out,
              attn_w_qkv, attn_w_o, mlp_w_in, mlp_w_out):
    for i in range(DEPTH):
        m, j = i % N_MIXERS, i // N_MIXERS
        h = _rmsnorm(x, norm_mix[i])
        if m == 0:
            y = _s5_mixer(h, ssm_a_re[j], ssm_a_im[j], ssm_b_re[j], ssm_b_im[j], ssm_c_re[j],
                          ssm_c_im[j], ssm_d[j], ssm_log_dt[j], ssm_w_glu[j])
        elif m == 1:
            y = _conv_mixer(h, conv_w_pw1[j], conv_b_pw1[j], conv_w_dw[j], conv_b_dw[j],
                            conv_ln_g[j], conv_ln_b[j], conv_w_pw2[j], conv_b_pw2[j])
        elif m == 2:
            y = _gmlp_mixer(h, gmlp_w_in[j], gmlp_ln_g[j], gmlp_ln_b[j], gmlp_w_s[j],
                            gmlp_b_s[j], gmlp_w_out[j])
        else:
            y = _attention_mixer(h, attn_w_qkv[j], attn_w_o[j])
        x = x + y.astype(x.dtype)
        x = x + _mlp(_rmsnorm(x, norm_mlp[i]), mlp_w_in[i], mlp_w_out[i]).astype(x.dtype)
    return _rmsnorm(x, norm_final)
```

```python
import functools
import math

import jax
import jax.numpy as jnp
from jax import lax
from jax.experimental import pallas as pl
from jax.experimental.pallas import tpu as pltpu

F32 = jnp.float32
BF16 = jnp.bfloat16

EPS = 1e-6
LANES = 128
VMEM_LIMIT = 56 * 1024 * 1024

SSM_GROUP = 16
SSM_STATE = 64
SSM_CHUNK = 16
GROUPS_PER_SLAB = LANES // SSM_GROUP
CONV_WIDTH = 31
CONV_HALO = 32
GMLP_CHUNK = 128
GMLP_HEADS = 4
ATT_CONFIGS = ((128, 1), (512, 4), (2048, 16))
ATT_HEADS = 8
HEAD_DIM = 64
ATT_WIDTH = ATT_HEADS * HEAD_DIM
ATT_BLOCK = 128


def _params(n_axes):
    return pltpu.CompilerParams(dimension_semantics=("arbitrary",) * n_axes,
                                vmem_limit_bytes=VMEM_LIMIT)


def _resident(shape):
    zeros = (0,) * len(shape)
    return pl.BlockSpec(shape, lambda *_: zeros, pipeline_mode=pl.Buffered(1))


def _rms(x, g):
    ms = jnp.mean(x * x, axis=-1, keepdims=True)
    return x * lax.rsqrt(ms + EPS) * g


def _layernorm(x, g, b):
    mu = jnp.mean(x, axis=-1, keepdims=True)
    xc = x - mu
    var = jnp.mean(xc * xc, axis=-1, keepdims=True)
    return xc * lax.rsqrt(var + EPS) * g + b


def _dot(a, b):
    return jnp.dot(a, b, preferred_element_type=F32)


def _mlp_kernel(x_ref, g_ref, win_ref, wout_ref, gf_ref, o_ref, *, final_norm):
    x = x_ref[...]
    h = _rms(x, g_ref[...]).astype(BF16)
    a = jnp.maximum(_dot(h, win_ref[...]), 0.0)
    a = (a * a).astype(BF16)
    y = x + _dot(a, wout_ref[...])
    if final_norm:
        y = _rms(y, gf_ref[...])
    o_ref[...] = y


def _mlp(x2, g, w_in, w_out, g_final, final_norm, tm=256):
    t, d = x2.shape
    dff = w_in.shape[1]
    row = pl.BlockSpec((tm, d), lambda i: (i, 0))
    return pl.pallas_call(
        functools.partial(_mlp_kernel, final_norm=final_norm),
        grid=(t // tm,),
        in_specs=[row, _resident((1, d)), _resident((d, dff)), _resident((dff, d)), _resident((1, d))],
        out_specs=row,
        out_shape=jax.ShapeDtypeStruct((t, d), F32),
        compiler_params=_params(1),
        name="mlp_final" if final_norm else "mlp",
    )(x2, g.reshape(1, d), w_in.astype(BF16), w_out.astype(BF16), g_final.reshape(1, d))


def _norm_kernel(x_ref, g_ref, o_ref):
    o_ref[...] = _rms(x_ref[...], g_ref[...])


def _norm(x2, g, tm=1024):
    t, d = x2.shape
    row = pl.BlockSpec((tm, d), lambda i: (i, 0))
    return pl.pallas_call(
        _norm_kernel, grid=(t // tm,), in_specs=[row, _resident((1, d))], out_specs=row,
        out_shape=jax.ShapeDtypeStruct((t, d), F32), compiler_params=_params(1), name="rmsnorm",
    )(x2, g.reshape(1, d))


def _s5_weights(a_re, a_im, b_re, b_im, c_re, c_im, log_dt):
    hp = lax.Precision.HIGHEST
    n_groups, n_state = a_re.shape
    L, gs = SSM_CHUNK, GROUPS_PER_SLAB
    n_slabs = n_groups // gs
    a = lax.complex(a_re, a_im)
    lam = a * jnp.exp(log_dt)[:, None]
    a_bar = jnp.exp(lam)
    b_bar = ((a_bar - 1.0) / a)[..., None] * lax.complex(b_re, b_im)
    c = lax.complex(c_re, c_im)
    steps = jnp.arange(L + 1, dtype=F32)
    pw = jnp.exp(lam[None] * steps[:, None, None])
    eye = jnp.eye(gs, dtype=F32)

    ro = c[None] * pw[1:, :, None, :]
    pw_rev = jnp.exp(lam[None] * (L - 1 - steps[:L])[:, None, None])
    inj = pw_rev[:, :, :, None] * b_bar[None]
    kern = jnp.real(jnp.einsum('gpn,lgnq->lgpq', c, pw[:L, :, :, None] * b_bar[None], precision=hp))

    def slabbed(t):
        return t.reshape((L, n_slabs, gs) + t.shape[2:])

    lag = jnp.arange(L)[None, :] - jnp.arange(L)[:, None]
    toe = slabbed(kern)[jnp.clip(lag, 0, L - 1)]
    toe = jnp.where((lag >= 0)[:, :, None, None, None, None], toe, 0.0)
    w_toe = jnp.einsum('jicgpq,gh->cjgqihp', toe, eye, precision=hp)
    w_toe = w_toe.reshape(n_slabs, L * LANES, L * LANES)

    inj2 = jnp.stack([jnp.real(inj), jnp.imag(inj)], axis=0)
    inj2 = inj2.reshape(2, L, n_slabs, gs, n_state, SSM_GROUP)
    w_inj = jnp.einsum('rjcgnq,gh->cjgqrhn', inj2, eye, precision=hp)
    w_inj = w_inj.reshape(n_slabs, L * LANES, 2 * gs * n_state)

    ro2 = jnp.stack([jnp.real(ro), -jnp.imag(ro)], axis=0)
    ro2 = ro2.reshape(2, L, n_slabs, gs, SSM_GROUP, n_state)
    w_ro = jnp.einsum('ricgpn,gh->crgnihp', ro2, eye, precision=hp)
    w_ro = w_ro.reshape(n_slabs, 2 * gs * n_state, L * LANES)

    a_chunk = pw[L].reshape(n_slabs, 1, gs * n_state)
    return (w_toe.astype(BF16), w_inj.astype(BF16), w_ro.astype(BF16),
            jnp.real(a_chunk), jnp.imag(a_chunk))


def _s5_kernel(u_ref, toe_ref, inj_ref, ro_ref, ar_ref, ai_ref, d_ref, y_ref, z_s, x_s):
    L = SSM_CHUNK
    n_chunks = u_ref.shape[0] // L
    half = ar_ref.shape[1]
    a_tok = [u_ref[pl.ds(j, n_chunks, stride=L), :] for j in range(L)]
    acat = jnp.concatenate([t.astype(BF16) for t in a_tok], axis=1)
    z_s[...] = _dot(acat, inj_ref[...])

    ar, ai = ar_ref[...], ai_ref[...]

    def step(c, carry):
        xr, xi = carry
        x_s[pl.ds(c, 1), :] = jnp.concatenate([xr, xi], axis=1)
        z = z_s[pl.ds(c, 1), :]
        return (ar * xr - ai * xi + z[:, :half], ar * xi + ai * xr + z[:, half:])

    zero = jnp.zeros((1, half), F32)
    lax.fori_loop(0, n_chunks, step, (zero, zero))
    x_in = x_s[...].astype(BF16)

    d = d_ref[...]
    pair = 2 * LANES
    for it in range(L // 2):
        k_hi = (it + 1) * pair
        cols = slice(it * pair, (it + 1) * pair)
        o = _dot(acat[:, :k_hi], toe_ref[:k_hi, cols]) + _dot(x_in, ro_ref[:, cols])
        for s in range(2):
            i = 2 * it + s
            y = o[:, s * LANES:(s + 1) * LANES] + d * a_tok[i]
            y_ref[pl.ds(i, n_chunks, stride=L), :] = jax.nn.gelu(y)


def _s5_scan(u3, weights, d_skip):
    bsz, s, d = u3.shape
    w_toe, w_inj, w_ro, a_r, a_i = weights
    n_slabs = d // LANES
    n_chunks = s // SSM_CHUNK
    n_st = a_r.shape[-1]
    slab_w = lambda shape: pl.BlockSpec((None,) + shape, lambda c, b: (c, 0, 0))
    act = pl.BlockSpec((None, s, LANES), lambda c, b: (b, 0, c))
    return pl.pallas_call(
        _s5_kernel,
        grid=(n_slabs, bsz),
        in_specs=[act, slab_w(w_toe.shape[1:]), slab_w(w_inj.shape[1:]), slab_w(w_ro.shape[1:]),
                  slab_w((1, n_st)), slab_w((1, n_st)), slab_w((1, LANES))],
        out_specs=act,
        out_shape=jax.ShapeDtypeStruct((bsz, s, d), F32),
        scratch_shapes=[pltpu.VMEM((n_chunks, 2 * n_st), F32), pltpu.VMEM((n_chunks, 2 * n_st), F32)],
        compiler_params=_params(2),
        name="s5_scan",
    )(u3, w_toe, w_inj, w_ro, a_r, a_i, d_skip.reshape(n_slabs, 1, LANES))


def _glu_kernel(x_ref, y_ref, w_ref, o_ref):
    d = x_ref.shape[1]
    z = _dot(y_ref[...].astype(BF16), w_ref[...])
    o_ref[...] = x_ref[...] + z[:, :d] * jax.nn.sigmoid(z[:, d:])


def _glu(x2, y2, w, tm=512):
    t, d = x2.shape
    row = pl.BlockSpec((tm, d), lambda i: (i, 0))
    return pl.pallas_call(
        _glu_kernel, grid=(t // tm,), in_specs=[row, row, _resident(w.shape)], out_specs=row,
        out_shape=jax.ShapeDtypeStruct((t, d), F32), compiler_params=_params(1), name="s5_glu",
    )(x2, y2, w.astype(BF16))


def _s5_mixer(x3, g, a_re, a_im, b_re, b_im, c_re, c_im, d_skip, log_dt, w_glu):
    bsz, s, d = x3.shape
    x2 = x3.reshape(bsz * s, d)
    u = _norm(x2, g)
    y = _s5_scan(u.reshape(bsz, s, d), _s5_weights(a_re, a_im, b_re, b_im, c_re, c_im, log_dt), d_skip)
    return _glu(x2, y.reshape(bsz * s, d), w_glu).reshape(bsz, s, d)


def _conv_kernel(x_ref, g_ref, w1_ref, b1_ref, wdw_ref, bdw_ref, lng_ref, lnb_ref, w2_ref, b2_ref,
                 o_ref, hist):
    ts, d = x_ref.shape
    t = pl.program_id(1)

    @pl.when(t == 0)
    def _():
        hist[pl.ds(0, CONV_HALO), :] = jnp.zeros((CONV_HALO, d), F32)

    @pl.when(t > 0)
    def _():
        hist[pl.ds(0, CONV_HALO), :] = hist[pl.ds(ts, CONV_HALO), :]

    x = x_ref[...]
    z = _dot(_rms(x, g_ref[...]).astype(BF16), w1_ref[...]) + b1_ref[...]
    hist[pl.ds(CONV_HALO, ts), :] = z[:, :d] * jax.nn.sigmoid(z[:, d:])

    first = CONV_HALO - (CONV_WIDTH - 1)
    acc = jnp.zeros((ts, d), F32)
    for k in range(CONV_WIDTH):
        acc = acc + wdw_ref[pl.ds(k, 1), :] * hist[pl.ds(first + k, ts), :]
    y = _layernorm(acc + bdw_ref[...], lng_ref[...], lnb_ref[...])
    y = (y * jax.nn.sigmoid(y)).astype(BF16)
    o_ref[...] = x + _dot(y, w2_ref[...]) + b2_ref[...]


def _conv_mixer(x3, g, w_pw1, b_pw1, w_dw, b_dw, ln_g, ln_b, w_pw2, b_pw2, ts=512):
    bsz, s, d = x3.shape
    act = pl.BlockSpec((None, ts, d), lambda b, t: (b, t, 0))
    vec = lambda v: v.reshape(1, -1)
    return pl.pallas_call(
        _conv_kernel,
        grid=(bsz, s // ts),
        in_specs=[act, _resident((1, d)), _resident((d, 2 * d)), _resident((1, 2 * d)),
                  _resident((CONV_WIDTH, d)), _resident((1, d)), _resident((1, d)), _resident((1, d)),
                  _resident((d, d)), _resident((1, d))],
        out_specs=act,
        out_shape=jax.ShapeDtypeStruct((bsz, s, d), F32),
        scratch_shapes=[pltpu.VMEM((ts + CONV_HALO, d), F32)],
        compiler_params=_params(2),
        name="conv_mixer",
    )(x3, vec(g), w_pw1.astype(BF16), vec(b_pw1), w_dw, vec(b_dw), vec(ln_g), vec(ln_b),
      w_pw2.astype(BF16), vec(b_pw2))


def _gmlp_kernel(x_ref, g_ref, win_ref, lng_ref, lnb_ref, ws_ref, bs_ref, wout_ref, o_ref):
    tm, d = x_ref.shape
    e = wout_ref.shape[0]
    he = e // GMLP_HEADS
    x = x_ref[...]
    z = jax.nn.gelu(_dot(_rms(x, g_ref[...]).astype(BF16), win_ref[...]))
    u = z[:, :e]
    v = _layernorm(z[:, e:], lng_ref[...], lnb_ref[...]).astype(BF16)
    rows = lax.broadcasted_iota(jnp.int32, (GMLP_CHUNK, GMLP_CHUNK), 0)
    cols = lax.broadcasted_iota(jnp.int32, (GMLP_CHUNK, GMLP_CHUNK), 1)
    ws = [jnp.where(rows >= cols, ws_ref[h], 0.0).astype(BF16) for h in range(GMLP_HEADS)]
    bs = [bs_ref[:, h:h + 1] for h in range(GMLP_HEADS)]
    mixed = []
    for ck in range(tm // GMLP_CHUNK):
        r0 = ck * GMLP_CHUNK
        mixed.append(jnp.concatenate(
            [_dot(ws[h], v[r0:r0 + GMLP_CHUNK, h * he:(h + 1) * he]) + bs[h] for h in range(GMLP_HEADS)],
            axis=1))
    gated = (u * jnp.concatenate(mixed, axis=0)).astype(BF16)
    o_ref[...] = x + _dot(gated, wout_ref[...])


def _gmlp_mixer(x2, g, w_in, ln_g, ln_b, w_s, b_s, w_out, tm=256):
    t, d = x2.shape
    e = w_out.shape[0]
    row = pl.BlockSpec((tm, d), lambda i: (i, 0))
    return pl.pallas_call(
        _gmlp_kernel,
        grid=(t // tm,),
        in_specs=[row, _resident((1, d)), _resident((d, 2 * e)), _resident((1, e)), _resident((1, e)),
                  _resident(w_s.shape), _resident((GMLP_CHUNK, GMLP_HEADS)), _resident((e, d))],
        out_specs=row,
        out_shape=jax.ShapeDtypeStruct((t, d), F32),
        compiler_params=_params(1),
        name="gmlp_mixer",
    )(x2, g.reshape(1, d), w_in.astype(BF16), ln_g.reshape(1, e), ln_b.reshape(1, e), w_s, b_s.T,
      w_out.astype(BF16))


def _qkv_kernel(x_ref, g_ref, w_ref, q_ref, k_ref, v_ref):
    h = _rms(x_ref[...], g_ref[...]).astype(BF16)
    z = _dot(h, w_ref[...])
    w = q_ref.shape[-1]
    q_ref[...] = (z[:, :w] * (HEAD_DIM ** -0.5)).astype(BF16)
    k_ref[...] = z[:, w:2 * w].astype(BF16)
    v_ref[...] = z[:, 2 * w:].astype(BF16)


def _qkv(x3, g, w_g, dil):
    bsz, s, d = x3.shape
    sub = s // dil
    tm = min(sub, 512)
    xs = x3.reshape(bsz, sub, dil * d)
    out = jax.ShapeDtypeStruct((bsz, dil, sub, ATT_WIDTH), BF16)
    ospec = pl.BlockSpec((None, None, tm, ATT_WIDTH), lambda b, r, m: (b, r, m, 0))
    return pl.pallas_call(
        _qkv_kernel,
        grid=(bsz, dil, sub // tm),
        in_specs=[pl.BlockSpec((None, tm, d), lambda b, r, m: (b, m, r)), _resident((1, d)),
                  _resident(w_g.shape)],
        out_specs=[ospec, ospec, ospec],
        out_shape=[out, out, out],
        compiler_params=_params(3),
        name=f"attn_qkv_d{dil}",
    )(xs, g.reshape(1, d), w_g)


def _att_kernel(q_ref, kp_ref, kc_ref, vp_ref, vc_ref, o_ref, l_ref):
    blk = ATT_BLOCK
    n = pl.program_id(2)
    q = q_ref[...]
    k = jnp.concatenate([kp_ref[...], kc_ref[...]], axis=0)
    v = jnp.concatenate([vp_ref[...], vc_ref[...]], axis=0)
    i_idx = lax.broadcasted_iota(jnp.int32, (blk, 2 * blk), 0)
    j_idx = lax.broadcasted_iota(jnp.int32, (blk, 2 * blk), 1)
    dist = i_idx + blk - j_idx
    valid = (dist >= 0) & (dist <= blk) & ((j_idx >= blk) | (n > 0))
    low = lax.broadcasted_iota(jnp.int32, (1, LANES), 1) < HEAD_DIM
    for pr in range(ATT_WIDTH // LANES):
        sl = slice(pr * LANES, (pr + 1) * LANES)
        qp, kp, vp = q[:, sl], k[:, sl], v[:, sl]
        res = []
        for sel in (low, jnp.logical_not(low)):
            qa = jnp.where(sel, qp, jnp.zeros_like(qp))
            sc = lax.dot_general(qa, kp, (((1,), (1,)), ((), ())), preferred_element_type=F32)
            sc = jnp.where(valid, sc, -jnp.inf)
            m = jnp.max(sc, axis=-1, keepdims=True)
            p = jnp.exp(sc - m)
            den = jnp.sum(p, axis=-1, keepdims=True)
            res.append((_dot(p.astype(BF16), vp) / den, m + jnp.log(den)))
        o_ref[:, sl] = jnp.where(low, res[0][0], res[1][0])
        l_ref[:, sl] = jnp.where(low, res[0][1], res[1][1])


def _att(q, k, v, dil):
    bsz, _, sub, w = q.shape
    nb = sub // ATT_BLOCK
    cur = pl.BlockSpec((None, None, ATT_BLOCK, w), lambda b, r, n: (b, r, n, 0))
    prev = pl.BlockSpec((None, None, ATT_BLOCK, w), lambda b, r, n: (b, r, jnp.maximum(n - 1, 0), 0))
    ospec = pl.BlockSpec((None, ATT_BLOCK, w), lambda b, r, n: (b, n, r))
    out = jax.ShapeDtypeStruct((bsz, sub, dil * w), F32)
    o, lse = pl.pallas_call(
        _att_kernel,
        grid=(bsz, dil, nb),
        in_specs=[cur, prev, cur, prev, cur],
        out_specs=[ospec, ospec],
        out_shape=[out, out],
        compiler_params=_params(3),
        name=f"attn_core_d{dil}",
    )(q, k, k, v, v)
    return o.reshape(bsz * sub * dil, w), lse.reshape(bsz * sub * dil, w)


def _merge_kernel(x_ref, o0_ref, o1_ref, o2_ref, l0_ref, l1_ref, l2_ref, wo_ref, out_ref):
    outs = (o0_ref[...], o1_ref[...], o2_ref[...])
    lses = (l0_ref[...], l1_ref[...], l2_ref[...])
    m = jnp.maximum(jnp.maximum(lses[0], lses[1]), lses[2])
    es = [jnp.exp(l - m) for l in lses]
    den = es[0] + es[1] + es[2]
    merged = (es[0] * outs[0] + es[1] * outs[1] + es[2] * outs[2]) / den
    out_ref[...] = x_ref[...] + _dot(merged.astype(BF16), wo_ref[...])


def _merge(x2, outs, lses, w_o, tm=512):
    t, d = x2.shape
    w = w_o.shape[0]
    row = pl.BlockSpec((tm, d), lambda i: (i, 0))
    head = pl.BlockSpec((tm, w), lambda i: (i, 0))
    return pl.pallas_call(
        _merge_kernel, grid=(t // tm,), in_specs=[row] + [head] * 6 + [_resident(w_o.shape)],
        out_specs=row, out_shape=jax.ShapeDtypeStruct((t, d), F32), compiler_params=_params(1),
        name="attn_merge",
    )(x2, *outs, *lses, w_o.astype(BF16))


def _attention_mixer(x3, g, w_qkv, w_o):
    bsz, s, d = x3.shape
    n_groups = len(ATT_CONFIGS)
    w3 = w_qkv.astype(BF16).reshape(d, 3, n_groups, ATT_WIDTH)
    outs, lses = [], []
    for gi, (window, dil) in enumerate(ATT_CONFIGS):
        assert window // dil == ATT_BLOCK and s % (dil * ATT_BLOCK) == 0
        q, k, v = _qkv(x3, g, w3[:, :, gi].reshape(d, 3 * ATT_WIDTH), dil)
        o, lse = _att(q, k, v, dil)
        outs.append(o)
        lses.append(lse)
    return _merge(x3.reshape(bsz * s, d), outs, lses, w_o).reshape(bsz, s, d)


def kernel(x, norm_mix, norm_mlp, norm_final, ssm_a_re, ssm_a_im, ssm_b_re, ssm_b_im, ssm_c_re, ssm_c_im, ssm_d, ssm_log_dt, ssm_w_glu, conv_w_pw1, conv_b_pw1, conv_w_dw, conv_b_dw, conv_ln_g, conv_ln_b, conv_w_pw2, conv_b_pw2, gmlp_w_in, gmlp_ln_g, gmlp_ln_b, gmlp_w_s, gmlp_b_s, gmlp_w_out, attn_w_qkv, attn_w_o, mlp_w_in, mlp_w_out):
    bsz, s, d = x.shape
    depth = norm_mix.shape[0]
    n_mixers = 4
    for i in range(depth):
        m, j = i % n_mixers, i // n_mixers
        if m == 0:
            x = _s5_mixer(x, norm_mix[i], ssm_a_re[j], ssm_a_im[j], ssm_b_re[j], ssm_b_im[j], ssm_c_re[j],
                          ssm_c_im[j], ssm_d[j], ssm_log_dt[j], ssm_w_glu[j])
        elif m == 1:
            x = _conv_mixer(x, norm_mix[i], conv_w_pw1[j], conv_b_pw1[j], conv_w_dw[j], conv_b_dw[j],
                            conv_ln_g[j], conv_ln_b[j], conv_w_pw2[j], conv_b_pw2[j])
        elif m == 2:
            x = _gmlp_mixer(x.reshape(bsz * s, d), norm_mix[i], gmlp_w_in[j], gmlp_ln_g[j], gmlp_ln_b[j],
                            gmlp_w_s[j], gmlp_b_s[j], gmlp_w_out[j]).reshape(bsz, s, d)
        else:
            x = _attention_mixer(x, norm_mix[i], attn_w_qkv[j], attn_w_o[j])
        x = _mlp(x.reshape(bsz * s, d), norm_mlp[i], mlp_w_in[i], mlp_w_out[i], norm_final,
                 final_norm=(i == depth - 1)).reshape(bsz, s, d)
    return x
```

```python
import functools

import jax
import jax.numpy as jnp
from jax import lax
from jax.experimental import pallas as pl
from jax.experimental.pallas import tpu as pltpu

F32 = jnp.float32
BF16 = jnp.bfloat16

EPS = 1e-6
LANES = 128
SUBLANES = 8
VMEM_LIMIT = 56 * 1024 * 1024

SSM_GROUP = 16
SSM_STATE = 64
SSM_CHUNK = 16
GROUPS_PER_SLAB = LANES // SSM_GROUP
CONV_WIDTH = 31
CONV_HALO = 32
CONV_ROWS = 128
CONV_PITCH = 2
GMLP_CHUNK = 128
GMLP_HEADS = 4
ATT_CONFIGS = ((128, 1), (512, 4), (2048, 16))
ATT_HEADS = 8
HEAD_DIM = 64
ATT_WIDTH = ATT_HEADS * HEAD_DIM
ATT_BLOCK = 128
ATT_PAIRS = ATT_WIDTH // LANES
ATT_UNROLL = 4


def _params(n_axes):
    return pltpu.CompilerParams(dimension_semantics=("arbitrary",) * n_axes,
                                vmem_limit_bytes=VMEM_LIMIT)


def _resident(shape):
    zeros = (0,) * len(shape)
    return pl.BlockSpec(shape, lambda *_: zeros, pipeline_mode=pl.Buffered(1))


def _rms(x, g):
    ms = jnp.mean(x * x, axis=-1, keepdims=True)
    return x * lax.rsqrt(ms + EPS) * g


def _layernorm(x, g, b):
    mu = jnp.mean(x, axis=-1, keepdims=True)
    xc = x - mu
    var = jnp.mean(xc * xc, axis=-1, keepdims=True)
    return xc * lax.rsqrt(var + EPS) * g + b


def _dot(a, b):
    return jnp.dot(a, b, preferred_element_type=F32)


def _mlp_kernel(*refs, pre, final_norm):
    if pre is None:
        x_ref, g_ref, win_ref, wout_ref, gf_ref, o_ref = refs
        x = x_ref[...]
    else:
        x_ref, y_ref, wpre_ref, g_ref, win_ref, wout_ref, gf_ref, o_ref = refs
        d = x_ref.shape[1]
        z = _dot(y_ref[...].astype(BF16), wpre_ref[...])
        x = x_ref[...] + (z[:, :d] * jax.nn.sigmoid(z[:, d:]) if pre == "glu" else z)
    h = _rms(x, g_ref[...]).astype(BF16)
    a = jnp.maximum(_dot(h, win_ref[...]), 0.0)
    a = (a * a).astype(BF16)
    y = x + _dot(a, wout_ref[...])
    if final_norm:
        y = _rms(y, gf_ref[...])
    o_ref[...] = y


def _mlp(x2, g, w_in, w_out, g_final, final_norm, pre=None, y2=None, w_pre=None, tm=256):
    t, d = x2.shape
    dff = w_in.shape[1]
    row = pl.BlockSpec((tm, d), lambda i: (i, 0))
    ins, specs = [x2], [row]
    if pre is not None:
        ins += [y2, w_pre.astype(BF16)]
        specs += [pl.BlockSpec((tm, y2.shape[1]), lambda i: (i, 0)), _resident(w_pre.shape)]
    ins += [g.reshape(1, d), w_in.astype(BF16), w_out.astype(BF16), g_final.reshape(1, d)]
    specs += [_resident((1, d)), _resident((d, dff)), _resident((dff, d)), _resident((1, d))]
    return pl.pallas_call(
        functools.partial(_mlp_kernel, pre=pre, final_norm=final_norm),
        grid=(t // tm,),
        in_specs=specs,
        out_specs=row,
        out_shape=jax.ShapeDtypeStruct((t, d), F32),
        compiler_params=_params(1),
        name="mlp" + ("_" + pre if pre else "") + ("_final" if final_norm else ""),
    )(*ins)


def _norm_kernel(x_ref, g_ref, o_ref):
    o_ref[...] = _rms(x_ref[...], g_ref[...])


def _norm(x2, g, tm=1024):
    t, d = x2.shape
    row = pl.BlockSpec((tm, d), lambda i: (i, 0))
    return pl.pallas_call(
        _norm_kernel, grid=(t // tm,), in_specs=[row, _resident((1, d))], out_specs=row,
        out_shape=jax.ShapeDtypeStruct((t, d), F32), compiler_params=_params(1), name="rmsnorm",
    )(x2, g.reshape(1, d))


def _s5_weights(a_re, a_im, b_re, b_im, c_re, c_im, log_dt):
    hp = lax.Precision.HIGHEST
    n_groups, n_state = a_re.shape
    L, gs = SSM_CHUNK, GROUPS_PER_SLAB
    n_slabs = n_groups // gs
    a = lax.complex(a_re, a_im)
    lam = a * jnp.exp(log_dt)[:, None]
    a_bar = jnp.exp(lam)
    b_bar = ((a_bar - 1.0) / a)[..., None] * lax.complex(b_re, b_im)
    c = lax.complex(c_re, c_im)
    steps = jnp.arange(L + 1, dtype=F32)
    pw = jnp.exp(lam[None] * steps[:, None, None])
    pw_rev = jnp.exp(lam[None] * (L - 1 - steps[:L])[:, None, None])

    kern = jnp.real(jnp.einsum('gpn,lgnq->lgpq', c, pw[:L, :, :, None] * b_bar[None], precision=hp))
    kq = kern.transpose(0, 1, 3, 2).reshape(L, n_slabs, gs, SSM_GROUP, SSM_GROUP)
    toe = jnp.einsum('lcgqp,gh->clgqhp', kq, jnp.eye(gs, dtype=F32), precision=hp)
    toe = toe.reshape(n_slabs, L, LANES, LANES)

    inj = pw_rev[:, :, :, None] * b_bar[None]
    inj = jnp.stack([jnp.real(inj), jnp.imag(inj)], axis=0)
    inj = inj.reshape(2, L, n_slabs, gs, n_state, SSM_GROUP).transpose(2, 0, 1, 3, 5, 4)
    inj = inj.reshape(n_slabs, 2, L, LANES, n_state)
    inj = jnp.concatenate([inj, inj], axis=-1)

    ro = c[None] * pw[1:, :, None, :]
    ro = jnp.stack([jnp.real(ro), -jnp.imag(ro)], axis=0)
    ro = ro.reshape(2, L, n_slabs, gs, SSM_GROUP, n_state).transpose(2, 0, 1, 5, 3, 4)
    ro = ro.reshape(n_slabs, 2, L, n_state, LANES)
    ro = jnp.concatenate([ro, ro], axis=-2)

    a_chunk = pw[L].reshape(n_slabs, 1, gs * n_state)
    return (toe.astype(BF16), inj.astype(BF16), ro.astype(BF16), jnp.real(a_chunk), jnp.imag(a_chunk))


def _s5_expand(toe_ref, inj_ref, ro_ref, toe_s, inj_s, ro_s):
    L = SSM_CHUNK
    half = inj_s.shape[1] // 2
    row = lax.broadcasted_iota(jnp.int32, (LANES, LANES), 0)
    lane = lax.broadcasted_iota(jnp.int32, (LANES, LANES), 1)
    zero = jnp.zeros((LANES, LANES), F32)
    blk = lambda k: pl.ds(k * LANES, LANES)
    for j in range(L):
        for i in range(j - (j % 2), L):
            toe_s[blk(j), blk(i)] = toe_ref[i - j] if i >= j else zero.astype(BF16)
    for ri in range(2):
        for gp in range(half // LANES):
            in_pair = (row // SSM_GROUP // 2 == gp) & (row // SSM_GROUP % 2 == lane // SSM_STATE)
            out_pair = lane // SSM_GROUP == 2 * gp + row // SSM_STATE
            for t in range(L):
                inj_s[blk(t), pl.ds(ri * half + gp * LANES, LANES)] = jnp.where(
                    in_pair, inj_ref[ri, t].astype(F32), zero).astype(BF16)
                ro_s[pl.ds(ri * half + gp * LANES, LANES), blk(t)] = jnp.where(
                    out_pair, ro_ref[ri, t].astype(F32), zero).astype(BF16)


def _s5_kernel(u_ref, toe_ref, inj_ref, ro_ref, ar_ref, ai_ref, d_ref, y_ref,
               toe_s, inj_s, ro_s, z_s, x_s):
    L = SSM_CHUNK
    n_chunks = u_ref.shape[0] // L
    half = ar_ref.shape[1]

    @pl.when(pl.program_id(1) == 0)
    def _():
        _s5_expand(toe_ref, inj_ref, ro_ref, toe_s, inj_s, ro_s)

    a_tok = [u_ref[pl.ds(j, n_chunks, stride=L), :] for j in range(L)]
    acat = jnp.concatenate([t.astype(BF16) for t in a_tok], axis=1)
    z_s[...] = _dot(acat, inj_s[...])

    ar, ai = ar_ref[...], ai_ref[...]

    def step(c, carry):
        xr, xi = carry
        x_s[pl.ds(c, 1), :] = jnp.concatenate([xr, xi], axis=1)
        z = z_s[pl.ds(c, 1), :]
        return (ar * xr - ai * xi + z[:, :half], ar * xi + ai * xr + z[:, half:])

    zero = jnp.zeros((1, half), F32)
    lax.fori_loop(0, n_chunks, step, (zero, zero), unroll=SUBLANES)
    x_in = x_s[...].astype(BF16)

    d = d_ref[...]
    pair = 2 * LANES
    for it in range(L // 2):
        k_hi = (it + 1) * pair
        cols = slice(it * pair, (it + 1) * pair)
        o = _dot(acat[:, :k_hi], toe_s[:k_hi, cols]) + _dot(x_in, ro_s[:, cols])
        for s in range(2):
            i = 2 * it + s
            y = o[:, s * LANES:(s + 1) * LANES] + d * a_tok[i]
            y_ref[pl.ds(i, n_chunks, stride=L), :] = jax.nn.gelu(y)


def _s5_scan(u3, weights, d_skip):
    bsz, s, d = u3.shape
    toe, inj, ro, a_r, a_i = weights
    n_slabs = d // LANES
    n_chunks = s // SSM_CHUNK
    n_st = a_r.shape[-1]
    cw = SSM_CHUNK * LANES

    def slab(shape):
        nd = len(shape)
        return pl.BlockSpec((None,) + tuple(shape), lambda c, b: (c,) + (0,) * nd)

    act = pl.BlockSpec((None, s, LANES), lambda c, b: (b, 0, c))
    return pl.pallas_call(
        _s5_kernel,
        grid=(n_slabs, bsz),
        in_specs=[act, slab(toe.shape[1:]), slab(inj.shape[1:]), slab(ro.shape[1:]),
                  slab((1, n_st)), slab((1, n_st)), slab((1, LANES))],
        out_specs=act,
        out_shape=jax.ShapeDtypeStruct((bsz, s, d), F32),
        scratch_shapes=[pltpu.VMEM((cw, cw), BF16), pltpu.VMEM((cw, 2 * n_st), BF16),
                        pltpu.VMEM((2 * n_st, cw), BF16),
                        pltpu.VMEM((n_chunks, 2 * n_st), F32), pltpu.VMEM((n_chunks, 2 * n_st), F32)],
        compiler_params=_params(2),
        name="s5_scan",
    )(u3, toe, inj, ro, a_r, a_i, d_skip.reshape(n_slabs, 1, LANES))


def _s5_mixer(x3, g, a_re, a_im, b_re, b_im, c_re, c_im, d_skip, log_dt):
    bsz, s, d = x3.shape
    u = _norm(x3.reshape(bsz * s, d), g)
    y = _s5_scan(u.reshape(bsz, s, d), _s5_weights(a_re, a_im, b_re, b_im, c_re, c_im, log_dt), d_skip)
    return y.reshape(bsz * s, d)


def _conv_kernel(x_ref, g_ref, w1_ref, b1_ref, wdw_ref, bdw_ref, lng_ref, lnb_ref, w2_ref, b2_ref,
                 o_ref, hist, conv_s):
    ts, d = x_ref.shape
    t = pl.program_id(1)
    n_slabs = d // LANES

    def hrows(r0, n):
        return pl.ds(CONV_PITCH * r0, n, stride=CONV_PITCH)

    @pl.when(t == 0)
    def _():
        for c in range(n_slabs):
            hist[c, hrows(0, CONV_HALO), :] = jnp.zeros((CONV_HALO, LANES), F32)

    @pl.when(t > 0)
    def _():
        for c in range(n_slabs):
            hist[c, hrows(0, CONV_HALO), :] = hist[c, hrows(ts, CONV_HALO), :]

    x = x_ref[...]
    z = _dot(_rms(x, g_ref[...]).astype(BF16), w1_ref[...]) + b1_ref[...]
    gated = z[:, :d] * jax.nn.sigmoid(z[:, d:])
    for c in range(n_slabs):
        hist[c, hrows(CONV_HALO, ts), :] = gated[:, c * LANES:(c + 1) * LANES]

    first = CONV_HALO - (CONV_WIDTH - 1)
    for c in range(n_slabs):
        lanes = pl.ds(c * LANES, LANES)
        for rb in range(ts // CONV_ROWS):
            acc = jnp.zeros((CONV_ROWS, LANES), F32)
            for k in range(CONV_WIDTH):
                acc = acc + wdw_ref[pl.ds(k, 1), lanes] * hist[c, hrows(rb * CONV_ROWS + first + k, CONV_ROWS), :]
            conv_s[pl.ds(rb * CONV_ROWS, CONV_ROWS), lanes] = acc
    y = _layernorm(conv_s[...] + bdw_ref[...], lng_ref[...], lnb_ref[...])
    y = (y * jax.nn.sigmoid(y)).astype(BF16)
    o_ref[...] = x + _dot(y, w2_ref[...]) + b2_ref[...]


def _conv_mixer(x3, g, w_pw1, b_pw1, w_dw, b_dw, ln_g, ln_b, w_pw2, b_pw2, ts=512):
    bsz, s, d = x3.shape
    act = pl.BlockSpec((None, ts, d), lambda b, t: (b, t, 0))
    vec = lambda v: v.reshape(1, -1)
    return pl.pallas_call(
        _conv_kernel,
        grid=(bsz, s // ts),
        in_specs=[act, _resident((1, d)), _resident((d, 2 * d)), _resident((1, 2 * d)),
                  _resident((CONV_WIDTH, d)), _resident((1, d)), _resident((1, d)), _resident((1, d)),
                  _resident((d, d)), _resident((1, d))],
        out_specs=act,
        out_shape=jax.ShapeDtypeStruct((bsz, s, d), F32),
        scratch_shapes=[pltpu.VMEM((d // LANES, CONV_PITCH * (ts + CONV_HALO), LANES), F32),
                        pltpu.VMEM((ts, d), F32)],
        compiler_params=_params(2),
        name="conv_mixer",
    )(x3, vec(g), w_pw1.astype(BF16), vec(b_pw1), w_dw, vec(b_dw), vec(ln_g), vec(ln_b),
      w_pw2.astype(BF16), vec(b_pw2))


def _gmlp_kernel(x_ref, g_ref, win_ref, lng_ref, lnb_ref, ws_ref, bs_ref, wout_ref, o_ref):
    tm, d = x_ref.shape
    e = wout_ref.shape[0]
    he = e // GMLP_HEADS
    x = x_ref[...]
    z = jax.nn.gelu(_dot(_rms(x, g_ref[...]).astype(BF16), win_ref[...]))
    u = z[:, :e]
    v = _layernorm(z[:, e:], lng_ref[...], lnb_ref[...]).astype(BF16)
    rows = lax.broadcasted_iota(jnp.int32, (GMLP_CHUNK, GMLP_CHUNK), 0)
    cols = lax.broadcasted_iota(jnp.int32, (GMLP_CHUNK, GMLP_CHUNK), 1)
    ws = [jnp.where(rows >= cols, ws_ref[h], 0.0).astype(BF16) for h in range(GMLP_HEADS)]
    bs = [bs_ref[:, h:h + 1] for h in range(GMLP_HEADS)]
    mixed = []
    for ck in range(tm // GMLP_CHUNK):
        r0 = ck * GMLP_CHUNK
        mixed.append(jnp.concatenate(
            [_dot(ws[h], v[r0:r0 + GMLP_CHUNK, h * he:(h + 1) * he]) + bs[h] for h in range(GMLP_HEADS)],
            axis=1))
    gated = (u * jnp.concatenate(mixed, axis=0)).astype(BF16)
    o_ref[...] = x + _dot(gated, wout_ref[...])


def _gmlp_mixer(x2, g, w_in, ln_g, ln_b, w_s, b_s, w_out, tm=256):
    t, d = x2.shape
    e = w_out.shape[0]
    row = pl.BlockSpec((tm, d), lambda i: (i, 0))
    return pl.pallas_call(
        _gmlp_kernel,
        grid=(t // tm,),
        in_specs=[row, _resident((1, d)), _resident((d, 2 * e)), _resident((1, e)), _resident((1, e)),
                  _resident(w_s.shape), _resident((GMLP_CHUNK, GMLP_HEADS)), _resident((e, d))],
        out_specs=row,
        out_shape=jax.ShapeDtypeStruct((t, d), F32),
        compiler_params=_params(1),
        name="gmlp_mixer",
    )(x2, g.reshape(1, d), w_in.astype(BF16), ln_g.reshape(1, e), ln_b.reshape(1, e), w_s, b_s.T,
      w_out.astype(BF16))


def _qkv_kernel(x_ref, g_ref, w_ref, o_ref):
    h = _rms(x_ref[...], g_ref[...]).astype(BF16)
    z = _dot(h, w_ref[...])
    nq = o_ref.shape[1] // 3
    o_ref[:, :nq] = (z[:, :nq] * (HEAD_DIM ** -0.5)).astype(BF16)
    o_ref[:, nq:] = z[:, nq:].astype(BF16)


def _qkv(x2, g, w_qkv, tm=256):
    t, d = x2.shape
    n = w_qkv.shape[1]
    return pl.pallas_call(
        _qkv_kernel,
        grid=(t // tm,),
        in_specs=[pl.BlockSpec((tm, d), lambda i: (i, 0)), _resident((1, d)), _resident((d, n))],
        out_specs=pl.BlockSpec((tm, n), lambda i: (i, 0)),
        out_shape=jax.ShapeDtypeStruct((t, n), BF16),
        compiler_params=_params(1),
        name="attn_qkv",
    )(x2, g.reshape(1, d), w_qkv.astype(BF16))


def _att_group(qf, kf, vf, og, lg, dil):
    blk = ATT_BLOCK
    s = qf.shape[0]
    nb = s // (dil * blk)

    def rows(start):
        return pl.ds(start, blk, stride=dil) if dil > 1 else pl.ds(pl.multiple_of(start, blk), blk)

    i_idx = lax.broadcasted_iota(jnp.int32, (blk, 2 * blk), 0)
    j_idx = lax.broadcasted_iota(jnp.int32, (blk, 2 * blk), 1)
    dist = i_idx + blk - j_idx
    band = (dist >= 0) & (dist <= blk)
    current = j_idx >= blk
    low = lax.broadcasted_iota(jnp.int32, (1, LANES), 1) < HEAD_DIM

    def body(idx, carry):
        r, n = idx // nb, idx % nb
        start = r + n * (blk * dil)
        prev = jnp.maximum(start - blk * dil, r)
        q = qf[rows(start), :].astype(BF16)
        k = jnp.concatenate([kf[rows(prev), :], kf[rows(start), :]], axis=0).astype(BF16)
        v = jnp.concatenate([vf[rows(prev), :], vf[rows(start), :]], axis=0).astype(BF16)
        valid = band & (current | (n > 0))
        res = []
        for sel in (low, jnp.logical_not(low)):
            qa = jnp.where(sel, q, jnp.zeros_like(q))
            sc = lax.dot_general(qa, k, (((1,), (1,)), ((), ())), preferred_element_type=F32)
            sc = jnp.where(valid, sc, -jnp.inf)
            m = jnp.max(sc, axis=-1, keepdims=True)
            p = jnp.exp(sc - m)
            den = jnp.sum(p, axis=-1, keepdims=True)
            res.append((_dot(p.astype(BF16), v) / den, m + jnp.log(den)))
        og[rows(start), :] = jnp.where(low, res[0][0], res[1][0])
        lg[rows(start), :] = jnp.where(low, res[0][1], res[1][1])
        return carry

    lax.fori_loop(0, dil * nb, body, 0, unroll=ATT_UNROLL)


def _att_kernel(q_ref, k_ref, v_ref, o_ref, qf, kf, vf, og0, og1, og2, lg0, lg1, lg2):
    g = pl.program_id(2)
    ogs, lgs = (og0, og1, og2), (lg0, lg1, lg2)
    qf[...] = q_ref[...].astype(F32)
    kf[...] = k_ref[...].astype(F32)
    vf[...] = v_ref[...].astype(F32)
    for gi, (_, dil) in enumerate(ATT_CONFIGS):
        @pl.when(g == gi)
        def _(gi=gi, dil=dil):
            _att_group(qf, kf, vf, ogs[gi], lgs[gi], dil)

    @pl.when(g == len(ATT_CONFIGS) - 1)
    def _():
        tile = 512

        def merge(t, carry):
            rs = pl.ds(pl.multiple_of(t * tile, tile), tile)
            l0, l1, l2 = lg0[rs, :], lg1[rs, :], lg2[rs, :]
            m = jnp.maximum(jnp.maximum(l0, l1), l2)
            e0, e1, e2 = jnp.exp(l0 - m), jnp.exp(l1 - m), jnp.exp(l2 - m)
            merged = (e0 * og0[rs, :] + e1 * og1[rs, :] + e2 * og2[rs, :]) / (e0 + e1 + e2)
            o_ref[rs, :] = merged.astype(BF16)
            return carry

        lax.fori_loop(0, o_ref.shape[0] // tile, merge, 0)


def _attention(qkv3):
    bsz, s, _ = qkv3.shape
    n_groups = len(ATT_CONFIGS)
    per = n_groups * ATT_PAIRS

    def slab(which):
        return pl.BlockSpec((None, s, LANES), lambda b, p, g: (b, 0, which * per + g * ATT_PAIRS + p))

    buf = pltpu.VMEM((s, LANES), F32)
    return pl.pallas_call(
        _att_kernel,
        grid=(bsz, ATT_PAIRS, n_groups),
        in_specs=[slab(0), slab(1), slab(2)],
        out_specs=pl.BlockSpec((None, s, LANES), lambda b, p, g: (b, 0, p)),
        out_shape=jax.ShapeDtypeStruct((bsz, s, ATT_WIDTH), BF16),
        scratch_shapes=[buf] * (3 + 2 * n_groups),
        compiler_params=_params(3),
        name="attn_core",
    )(qkv3, qkv3, qkv3)


def _attention_mixer(x3, g, w_qkv):
    bsz, s, d = x3.shape
    for window, dil in ATT_CONFIGS:
        assert window // dil == ATT_BLOCK and s % (dil * ATT_BLOCK) == 0
    qkv = _qkv(x3.reshape(bsz * s, d), g, w_qkv)
    return _attention(qkv.reshape(bsz, s, -1)).reshape(bsz * s, ATT_WIDTH)


def _layer(x, seq_shape, i, p, final_norm):
    bsz, s = seq_shape
    d = x.shape[1]
    m, j = i % 4, i // 4
    x3 = x.reshape(bsz, s, d)
    pre, y, w_pre = None, None, None
    if m == 0:
        y = _s5_mixer(x3, p["norm_mix"][i], p["ssm_a_re"][j], p["ssm_a_im"][j], p["ssm_b_re"][j],
                      p["ssm_b_im"][j], p["ssm_c_re"][j], p["ssm_c_im"][j], p["ssm_d"][j], p["ssm_log_dt"][j])
        pre, w_pre = "glu", p["ssm_w_glu"][j]
    elif m == 1:
        x = _conv_mixer(x3, p["norm_mix"][i], p["conv_w_pw1"][j], p["conv_b_pw1"][j], p["conv_w_dw"][j],
                        p["conv_b_dw"][j], p["conv_ln_g"][j], p["conv_ln_b"][j], p["conv_w_pw2"][j],
                        p["conv_b_pw2"][j]).reshape(bsz * s, d)
    elif m == 2:
        x = _gmlp_mixer(x, p["norm_mix"][i], p["gmlp_w_in"][j], p["gmlp_ln_g"][j], p["gmlp_ln_b"][j],
                        p["gmlp_w_s"][j], p["gmlp_b_s"][j], p["gmlp_w_out"][j])
    else:
        y = _attention_mixer(x3, p["norm_mix"][i], p["attn_w_qkv"][j])
        pre, w_pre = "proj", p["attn_w_o"][j]
    return _mlp(x, p["norm_mlp"][i], p["mlp_w_in"][i], p["mlp_w_out"][i], p["norm_final"],
                final_norm=final_norm, pre=pre, y2=y, w_pre=w_pre)


def kernel(x, norm_mix, norm_mlp, norm_final, ssm_a_re, ssm_a_im, ssm_b_re, ssm_b_im, ssm_c_re, ssm_c_im, ssm_d, ssm_log_dt, ssm_w_glu, conv_w_pw1, conv_b_pw1, conv_w_dw, conv_b_dw, conv_ln_g, conv_ln_b, conv_w_pw2, conv_b_pw2, gmlp_w_in, gmlp_ln_g, gmlp_ln_b, gmlp_w_s, gmlp_b_s, gmlp_w_out, attn_w_qkv, attn_w_o, mlp_w_in, mlp_w_out):
    p = dict(locals())
    bsz, s, d = x.shape
    depth = norm_mix.shape[0]
    x = x.reshape(bsz * s, d)
    for i in range(depth):
        x = _layer(x, (bsz, s), i, p, final_norm=(i == depth - 1))
    return x.reshape(bsz, s, d)
```

```python
import functools

import jax
import jax.numpy as jnp
from jax import lax
from jax.experimental import pallas as pl
from jax.experimental.pallas import tpu as pltpu

F32 = jnp.float32
BF16 = jnp.bfloat16

EPS = 1e-6
LANES = 128
SUBLANES = 8
VMEM_LIMIT = 56 * 1024 * 1024

SSM_GROUP = 16
SSM_STATE = 64
SSM_CHUNK = 16
GROUPS_PER_SLAB = LANES // SSM_GROUP
CONV_WIDTH = 31
CONV_HALO = 32
CONV_ROWS = 128
CONV_PITCH = 2
GMLP_CHUNK = 128
GMLP_HEADS = 4
ATT_CONFIGS = ((128, 1), (512, 4), (2048, 16))
ATT_HEADS = 8
HEAD_DIM = 64
ATT_WIDTH = ATT_HEADS * HEAD_DIM
ATT_BLOCK = 128
ATT_PAIRS = ATT_WIDTH // LANES
ATT_UNROLL = 8


def _params(n_axes):
    return pltpu.CompilerParams(dimension_semantics=("arbitrary",) * n_axes,
                                vmem_limit_bytes=VMEM_LIMIT)


def _resident(shape):
    zeros = (0,) * len(shape)
    return pl.BlockSpec(shape, lambda *_: zeros, pipeline_mode=pl.Buffered(1))


def _rms(x, g):
    ms = jnp.mean(x * x, axis=-1, keepdims=True)
    return x * lax.rsqrt(ms + EPS) * g


def _layernorm(x, g, b):
    mu = jnp.mean(x, axis=-1, keepdims=True)
    xc = x - mu
    var = jnp.mean(xc * xc, axis=-1, keepdims=True)
    return xc * lax.rsqrt(var + EPS) * g + b


def _dot(a, b):
    return jnp.dot(a, b, preferred_element_type=F32)


def _mlp_kernel(*refs, pre, final_norm):
    if pre is None:
        x_ref, g_ref, win_ref, wout_ref, gf_ref, o_ref = refs
        x = x_ref[...]
    else:
        x_ref, y_ref, wpre_ref, g_ref, win_ref, wout_ref, gf_ref, o_ref = refs
        d = x_ref.shape[1]
        z = _dot(y_ref[...].astype(BF16), wpre_ref[...])
        x = x_ref[...] + (z[:, :d] * jax.nn.sigmoid(z[:, d:]) if pre == "glu" else z)
    h = _rms(x, g_ref[...]).astype(BF16)
    a = jnp.maximum(_dot(h, win_ref[...]), 0.0)
    a = (a * a).astype(BF16)
    y = x + _dot(a, wout_ref[...])
    if final_norm:
        y = _rms(y, gf_ref[...])
    o_ref[...] = y


def _mlp(x2, g, w_in, w_out, g_final, final_norm, pre=None, y2=None, w_pre=None, tm=512):
    t, d = x2.shape
    dff = w_in.shape[1]
    row = pl.BlockSpec((tm, d), lambda i: (i, 0))
    ins, specs = [x2], [row]
    if pre is not None:
        ins += [y2, w_pre.astype(BF16)]
        specs += [pl.BlockSpec((tm, y2.shape[1]), lambda i: (i, 0)), _resident(w_pre.shape)]
    ins += [g.reshape(1, d), w_in.astype(BF16), w_out.astype(BF16), g_final.reshape(1, d)]
    specs += [_resident((1, d)), _resident((d, dff)), _resident((dff, d)), _resident((1, d))]
    return pl.pallas_call(
        functools.partial(_mlp_kernel, pre=pre, final_norm=final_norm),
        grid=(t // tm,),
        in_specs=specs,
        out_specs=row,
        out_shape=jax.ShapeDtypeStruct((t, d), F32),
        compiler_params=_params(1),
        name="mlp" + ("_" + pre if pre else "") + ("_final" if final_norm else ""),
    )(*ins)


def _norm_kernel(x_ref, g_ref, o_ref):
    o_ref[...] = _rms(x_ref[...], g_ref[...])


def _norm(x2, g, tm=1024):
    t, d = x2.shape
    row = pl.BlockSpec((tm, d), lambda i: (i, 0))
    return pl.pallas_call(
        _norm_kernel, grid=(t // tm,), in_specs=[row, _resident((1, d))], out_specs=row,
        out_shape=jax.ShapeDtypeStruct((t, d), F32), compiler_params=_params(1), name="rmsnorm",
    )(x2, g.reshape(1, d))


def _s5_weights(a_re, a_im, b_re, b_im, c_re, c_im, log_dt):
    hp = lax.Precision.HIGHEST
    n_groups, n_state = a_re.shape
    L, gs = SSM_CHUNK, GROUPS_PER_SLAB
    n_slabs = n_groups // gs
    a = lax.complex(a_re, a_im)
    lam = a * jnp.exp(log_dt)[:, None]
    a_bar = jnp.exp(lam)
    b_bar = ((a_bar - 1.0) / a)[..., None] * lax.complex(b_re, b_im)
    c = lax.complex(c_re, c_im)
    steps = jnp.arange(L + 1, dtype=F32)
    pw = jnp.exp(lam[None] * steps[:, None, None])
    pw_rev = jnp.exp(lam[None] * (L - 1 - steps[:L])[:, None, None])

    kern = jnp.real(jnp.einsum('gpn,lgnq->lgpq', c, pw[:L, :, :, None] * b_bar[None], precision=hp))
    kq = kern.transpose(0, 1, 3, 2).reshape(L, n_slabs, gs, SSM_GROUP, SSM_GROUP)
    toe = jnp.einsum('lcgqp,gh->clgqhp', kq, jnp.eye(gs, dtype=F32), precision=hp)
    toe = toe.reshape(n_slabs, L, LANES, LANES)

    inj = pw_rev[:, :, :, None] * b_bar[None]
    inj = jnp.stack([jnp.real(inj), jnp.imag(inj)], axis=0)
    inj = inj.reshape(2, L, n_slabs, gs, n_state, SSM_GROUP).transpose(2, 0, 1, 3, 5, 4)
    inj = inj.reshape(n_slabs, 2, L, LANES, n_state)
    inj = jnp.concatenate([inj, inj], axis=-1)

    ro = c[None] * pw[1:, :, None, :]
    ro = jnp.stack([jnp.real(ro), -jnp.imag(ro)], axis=0)
    ro = ro.reshape(2, L, n_slabs, gs, SSM_GROUP, n_state).transpose(2, 0, 1, 5, 3, 4)
    ro = ro.reshape(n_slabs, 2, L, n_state, LANES)
    ro = jnp.concatenate([ro, ro], axis=-2)

    a_chunk = pw[L].reshape(n_slabs, 1, gs * n_state)
    return (toe.astype(BF16), inj.astype(BF16), ro.astype(BF16), jnp.real(a_chunk), jnp.imag(a_chunk))


def _s5_expand(toe_ref, inj_ref, ro_ref, toe_s, inj_s, ro_s):
    L = SSM_CHUNK
    half = inj_s.shape[1] // 2
    row = lax.broadcasted_iota(jnp.int32, (LANES, LANES), 0)
    lane = lax.broadcasted_iota(jnp.int32, (LANES, LANES), 1)
    zero = jnp.zeros((LANES, LANES), F32)
    blk = lambda k: pl.ds(k * LANES, LANES)
    for j in range(L):
        for i in range(j - (j % 2), L):
            toe_s[blk(j), blk(i)] = toe_ref[i - j] if i >= j else zero.astype(BF16)
    for ri in range(2):
        for gp in range(half // LANES):
            in_pair = (row // SSM_GROUP // 2 == gp) & (row // SSM_GROUP % 2 == lane // SSM_STATE)
            out_pair = lane // SSM_GROUP == 2 * gp + row // SSM_STATE
            for t in range(L):
                inj_s[blk(t), pl.ds(ri * half + gp * LANES, LANES)] = jnp.where(
                    in_pair, inj_ref[ri, t].astype(F32), zero).astype(BF16)
                ro_s[pl.ds(ri * half + gp * LANES, LANES), blk(t)] = jnp.where(
                    out_pair, ro_ref[ri, t].astype(F32), zero).astype(BF16)


def _s5_kernel(u_ref, toe_ref, inj_ref, ro_ref, ar_ref, ai_ref, d_ref, y_ref,
               toe_s, inj_s, ro_s, z_s, x_s):
    L = SSM_CHUNK
    n_chunks = u_ref.shape[0] // L
    half = ar_ref.shape[1]

    @pl.when(pl.program_id(1) == 0)
    def _():
        _s5_expand(toe_ref, inj_ref, ro_ref, toe_s, inj_s, ro_s)

    a_tok = [u_ref[pl.ds(j, n_chunks, stride=L), :] for j in range(L)]
    acat = jnp.concatenate([t.astype(BF16) for t in a_tok], axis=1)
    z_s[...] = _dot(acat, inj_s[...])

    ar, ai = ar_ref[...], ai_ref[...]

    def step(c, carry):
        xr, xi = carry
        x_s[pl.ds(c, 1), :] = jnp.concatenate([xr, xi], axis=1)
        z = z_s[pl.ds(c, 1), :]
        return (ar * xr - ai * xi + z[:, :half], ar * xi + ai * xr + z[:, half:])

    zero = jnp.zeros((1, half), F32)
    lax.fori_loop(0, n_chunks, step, (zero, zero), unroll=SUBLANES)
    x_in = x_s[...].astype(BF16)

    d = d_ref[...]
    pair = 2 * LANES
    for it in range(L // 2):
        k_hi = (it + 1) * pair
        cols = slice(it * pair, (it + 1) * pair)
        o = _dot(acat[:, :k_hi], toe_s[:k_hi, cols]) + _dot(x_in, ro_s[:, cols])
        for s in range(2):
            i = 2 * it + s
            y = o[:, s * LANES:(s + 1) * LANES] + d * a_tok[i]
            y_ref[pl.ds(i, n_chunks, stride=L), :] = jax.nn.gelu(y)


def _s5_scan(u3, weights, d_skip):
    bsz, s, d = u3.shape
    toe, inj, ro, a_r, a_i = weights
    n_slabs = d // LANES
    n_chunks = s // SSM_CHUNK
    n_st = a_r.shape[-1]
    cw = SSM_CHUNK * LANES

    def slab(shape):
        nd = len(shape)
        return pl.BlockSpec((None,) + tuple(shape), lambda c, b: (c,) + (0,) * nd)

    act = pl.BlockSpec((None, s, LANES), lambda c, b: (b, 0, c))
    return pl.pallas_call(
        _s5_kernel,
        grid=(n_slabs, bsz),
        in_specs=[act, slab(toe.shape[1:]), slab(inj.shape[1:]), slab(ro.shape[1:]),
                  slab((1, n_st)), slab((1, n_st)), slab((1, LANES))],
        out_specs=act,
        out_shape=jax.ShapeDtypeStruct((bsz, s, d), F32),
        scratch_shapes=[pltpu.VMEM((cw, cw), BF16), pltpu.VMEM((cw, 2 * n_st), BF16),
                        pltpu.VMEM((2 * n_st, cw), BF16),
                        pltpu.VMEM((n_chunks, 2 * n_st), F32), pltpu.VMEM((n_chunks, 2 * n_st), F32)],
        compiler_params=_params(2),
        name="s5_scan",
    )(u3, toe, inj, ro, a_r, a_i, d_skip.reshape(n_slabs, 1, LANES))


def _s5_mixer(x3, g, a_re, a_im, b_re, b_im, c_re, c_im, d_skip, log_dt):
    bsz, s, d = x3.shape
    u = _norm(x3.reshape(bsz * s, d), g)
    y = _s5_scan(u.reshape(bsz, s, d), _s5_weights(a_re, a_im, b_re, b_im, c_re, c_im, log_dt), d_skip)
    return y.reshape(bsz * s, d)


def _conv_kernel(x_ref, g_ref, w1_ref, b1_ref, wdw_ref, bdw_ref, lng_ref, lnb_ref, w2_ref, b2_ref,
                 o_ref, hist, conv_s):
    ts, d = x_ref.shape
    t = pl.program_id(1)
    n_slabs = d // LANES

    def hrows(r0, n):
        return pl.ds(CONV_PITCH * r0, n, stride=CONV_PITCH)

    @pl.when(t == 0)
    def _():
        for c in range(n_slabs):
            hist[c, hrows(0, CONV_HALO), :] = jnp.zeros((CONV_HALO, LANES), F32)

    @pl.when(t > 0)
    def _():
        for c in range(n_slabs):
            hist[c, hrows(0, CONV_HALO), :] = hist[c, hrows(ts, CONV_HALO), :]

    x = x_ref[...]
    z = _dot(_rms(x, g_ref[...]).astype(BF16), w1_ref[...]) + b1_ref[...]
    gated = z[:, :d] * jax.nn.sigmoid(z[:, d:])
    for c in range(n_slabs):
        hist[c, hrows(CONV_HALO, ts), :] = gated[:, c * LANES:(c + 1) * LANES]

    first = CONV_HALO - (CONV_WIDTH - 1)
    for c in range(n_slabs):
        lanes = pl.ds(c * LANES, LANES)
        for rb in range(ts // CONV_ROWS):
            acc = jnp.zeros((CONV_ROWS, LANES), F32)
            for k in range(CONV_WIDTH):
                acc = acc + wdw_ref[pl.ds(k, 1), lanes] * hist[c, hrows(rb * CONV_ROWS + first + k, CONV_ROWS), :]
            conv_s[pl.ds(rb * CONV_ROWS, CONV_ROWS), lanes] = acc
    y = _layernorm(conv_s[...] + bdw_ref[...], lng_ref[...], lnb_ref[...])
    y = (y * jax.nn.sigmoid(y)).astype(BF16)
    o_ref[...] = x + _dot(y, w2_ref[...]) + b2_ref[...]


def _conv_mixer(x3, g, w_pw1, b_pw1, w_dw, b_dw, ln_g, ln_b, w_pw2, b_pw2, ts=512):
    bsz, s, d = x3.shape
    act = pl.BlockSpec((None, ts, d), lambda b, t: (b, t, 0))
    vec = lambda v: v.reshape(1, -1)
    return pl.pallas_call(
        _conv_kernel,
        grid=(bsz, s // ts),
        in_specs=[act, _resident((1, d)), _resident((d, 2 * d)), _resident((1, 2 * d)),
                  _resident((CONV_WIDTH, d)), _resident((1, d)), _resident((1, d)), _resident((1, d)),
                  _resident((d, d)), _resident((1, d))],
        out_specs=act,
        out_shape=jax.ShapeDtypeStruct((bsz, s, d), F32),
        scratch_shapes=[pltpu.VMEM((d // LANES, CONV_PITCH * (ts + CONV_HALO), LANES), F32),
                        pltpu.VMEM((ts, d), F32)],
        compiler_params=_params(2),
        name="conv_mixer",
    )(x3, vec(g), w_pw1.astype(BF16), vec(b_pw1), w_dw, vec(b_dw), vec(ln_g), vec(ln_b),
      w_pw2.astype(BF16), vec(b_pw2))


def _gmlp_kernel(x_ref, g_ref, win_ref, lng_ref, lnb_ref, ws_ref, bs_ref, wout_ref, o_ref):
    tm, d = x_ref.shape
    e = wout_ref.shape[0]
    he = e // GMLP_HEADS
    x = x_ref[...]
    z = jax.nn.gelu(_dot(_rms(x, g_ref[...]).astype(BF16), win_ref[...]))
    u = z[:, :e]
    v = _layernorm(z[:, e:], lng_ref[...], lnb_ref[...]).astype(BF16)
    rows = lax.broadcasted_iota(jnp.int32, (GMLP_CHUNK, GMLP_CHUNK), 0)
    cols = lax.broadcasted_iota(jnp.int32, (GMLP_CHUNK, GMLP_CHUNK), 1)
    ws = [jnp.where(rows >= cols, ws_ref[h], 0.0).astype(BF16) for h in range(GMLP_HEADS)]
    bs = [bs_ref[:, h:h + 1] for h in range(GMLP_HEADS)]
    mixed = []
    for ck in range(tm // GMLP_CHUNK):
        r0 = ck * GMLP_CHUNK
        mixed.append(jnp.concatenate(
            [_dot(ws[h], v[r0:r0 + GMLP_CHUNK, h * he:(h + 1) * he]) + bs[h] for h in range(GMLP_HEADS)],
            axis=1))
    gated = (u * jnp.concatenate(mixed, axis=0)).astype(BF16)
    o_ref[...] = x + _dot(gated, wout_ref[...])


def _gmlp_mixer(x2, g, w_in, ln_g, ln_b, w_s, b_s, w_out, tm=512):
    t, d = x2.shape
    e = w_out.shape[0]
    row = pl.BlockSpec((tm, d), lambda i: (i, 0))
    return pl.pallas_call(
        _gmlp_kernel,
        grid=(t // tm,),
        in_specs=[row, _resident((1, d)), _resident((d, 2 * e)), _resident((1, e)), _resident((1, e)),
                  _resident(w_s.shape), _resident((GMLP_CHUNK, GMLP_HEADS)), _resident((e, d))],
        out_specs=row,
        out_shape=jax.ShapeDtypeStruct((t, d), F32),
        compiler_params=_params(1),
        name="gmlp_mixer",
    )(x2, g.reshape(1, d), w_in.astype(BF16), ln_g.reshape(1, e), ln_b.reshape(1, e), w_s, b_s.T,
      w_out.astype(BF16))


def _qkv_kernel(x_ref, g_ref, w_ref, *refs):
    outs, zs = refs[:-1], refs[-1]
    tm = x_ref.shape[0]
    h = _rms(x_ref[...], g_ref[...]).astype(BF16)
    gw = 3 * ATT_WIDTH
    for gi, (_, dil) in sorted(enumerate(ATT_CONFIGS), key=lambda e: -e[1][1]):
        zg = _dot(h, w_ref[:, gi * gw:(gi + 1) * gw])
        zg = jnp.concatenate([zg[:, :ATT_WIDTH] * (HEAD_DIM ** -0.5), zg[:, ATT_WIDTH:]], axis=1)
        if dil == 1:
            outs[gi][0] = zg.astype(BF16)
            continue
        for c in range(gw // LANES):
            zs[gi, c] = zg[:, c * LANES:(c + 1) * LANES]
        for r in range(dil):
            outs[gi][r] = jnp.concatenate(
                [zs[gi, c, pl.ds(r, tm // dil, stride=dil), :] for c in range(gw // LANES)], axis=1).astype(BF16)


def _qkv(x3, g, w_qkv, tm=512):
    bsz, s, d = x3.shape
    n_groups = len(ATT_CONFIGS)
    gw = 3 * ATT_WIDTH
    w = w_qkv.astype(BF16).reshape(d, 3, n_groups, ATT_WIDTH).transpose(0, 2, 1, 3).reshape(d, n_groups * gw)
    outs = [jax.ShapeDtypeStruct((bsz, dil, s // dil, gw), BF16) for _, dil in ATT_CONFIGS]
    ospecs = [pl.BlockSpec((None, dil, tm // dil, gw), lambda b, i: (b, 0, i, 0)) for _, dil in ATT_CONFIGS]
    return pl.pallas_call(
        _qkv_kernel,
        grid=(bsz, s // tm),
        in_specs=[pl.BlockSpec((None, tm, d), lambda b, i: (b, i, 0)), _resident((1, d)), _resident(w.shape)],
        out_specs=ospecs,
        out_shape=outs,
        scratch_shapes=[pltpu.VMEM((n_groups, gw // LANES, tm, LANES), F32)],
        compiler_params=_params(2),
        name="attn_qkv",
    )(x3, g.reshape(1, d), w)


def _att_group(q_ref, k_ref, v_ref, bias_ref, og, lg, dil):
    blk = ATT_BLOCK
    s = q_ref.shape[0]
    nb = s // (dil * blk)

    low = lax.broadcasted_iota(jnp.int32, (1, LANES), 1) < HEAD_DIM

    def body(idx, carry):
        r, n = idx // nb, idx % nb
        cur = pl.ds(pl.multiple_of(idx * blk, blk), blk)
        prev = pl.ds(pl.multiple_of((idx - jnp.minimum(n, 1)) * blk, blk), blk)
        out_rows = pl.ds(r + n * (blk * dil), blk, stride=dil) if dil > 1 else cur
        q = q_ref[cur, :]
        k = jnp.concatenate([k_ref[prev, :], k_ref[cur, :]], axis=0)
        v = jnp.concatenate([v_ref[prev, :], v_ref[cur, :]], axis=0)
        bias = bias_ref[jnp.minimum(n, 1)]
        qq = jnp.concatenate([jnp.where(low, q, jnp.zeros_like(q)), jnp.where(low, jnp.zeros_like(q), q)], axis=0)
        sc = lax.dot_general(qq, k, (((1,), (1,)), ((), ())), preferred_element_type=F32) + bias
        m = jnp.max(sc, axis=-1, keepdims=True)
        p = jnp.exp(sc - m).astype(BF16)
        pv = _dot(p, jnp.concatenate([v, jnp.ones_like(v)], axis=1))
        num = jnp.where(low, pv[:blk, :LANES], pv[blk:, :LANES])
        den = jnp.where(low, pv[:blk, LANES:], pv[blk:, LANES:])
        og[out_rows, :] = num / den
        lg[out_rows, :] = jnp.where(low, m[:blk], m[blk:]) + jnp.log(den)
        return carry

    lax.fori_loop(0, dil * nb, body, 0, unroll=ATT_UNROLL)


def _att_kernel(*refs):
    n_groups = len(ATT_CONFIGS)
    qkv, bias_ref, o_ref = refs[:3 * n_groups], refs[3 * n_groups], refs[3 * n_groups + 1]
    ogs = refs[3 * n_groups + 2:4 * n_groups + 2]
    lgs = refs[4 * n_groups + 2:]
    for gi, (_, dil) in enumerate(ATT_CONFIGS):
        _att_group(qkv[3 * gi], qkv[3 * gi + 1], qkv[3 * gi + 2], bias_ref, ogs[gi], lgs[gi], dil)

    tile = 512

    def merge(t, carry):
        rs = pl.ds(pl.multiple_of(t * tile, tile), tile)
        ls = [lg[rs, :] for lg in lgs]
        m = functools.reduce(jnp.maximum, ls)
        es = [jnp.exp(l - m) for l in ls]
        merged = sum(e * og[rs, :] for e, og in zip(es, ogs)) / sum(es)
        o_ref[rs, :] = merged.astype(BF16)
        return carry

    lax.fori_loop(0, o_ref.shape[0] // tile, merge, 0)


def _attention(qkv_groups):
    bsz, _, _, gw = qkv_groups[0].shape
    s = qkv_groups[0].shape[1] * qkv_groups[0].shape[2]
    per = ATT_WIDTH // LANES
    ins, specs = [], []
    for a in qkv_groups:
        for which in range(3):
            ins.append(a.reshape(bsz, s, gw))
            specs.append(pl.BlockSpec((None, s, LANES), lambda b, p, which=which: (b, 0, which * per + p)))
    i_idx = lax.broadcasted_iota(jnp.int32, (ATT_BLOCK, 2 * ATT_BLOCK), 0)
    j_idx = lax.broadcasted_iota(jnp.int32, (ATT_BLOCK, 2 * ATT_BLOCK), 1)
    dist = i_idx + ATT_BLOCK - j_idx
    band = (dist >= 0) & (dist <= ATT_BLOCK)
    bias = jnp.where(jnp.stack([band & (j_idx >= ATT_BLOCK), band]), 0.0, -jnp.inf).astype(F32)
    bias = jnp.concatenate([bias, bias], axis=1)
    buf = pltpu.VMEM((s, LANES), F32)
    return pl.pallas_call(
        _att_kernel,
        grid=(bsz, ATT_PAIRS),
        in_specs=specs + [_resident(bias.shape)],
        out_specs=pl.BlockSpec((None, s, LANES), lambda b, p: (b, 0, p)),
        out_shape=jax.ShapeDtypeStruct((bsz, s, ATT_WIDTH), BF16),
        scratch_shapes=[buf] * (2 * len(ATT_CONFIGS)),
        compiler_params=_params(2),
        name="attn_core",
    )(*ins, bias)


def _attention_mixer(x3, g, w_qkv):
    bsz, s, d = x3.shape
    for window, dil in ATT_CONFIGS:
        assert window // dil == ATT_BLOCK and s % (dil * ATT_BLOCK) == 0
    return _attention(_qkv(x3, g, w_qkv)).reshape(bsz * s, ATT_WIDTH)


def _layer(x, seq_shape, i, p, final_norm):
    bsz, s = seq_shape
    d = x.shape[1]
    m, j = i % 4, i // 4
    x3 = x.reshape(bsz, s, d)
    pre, y, w_pre = None, None, None
    if m == 0:
        y = _s5_mixer(x3, p["norm_mix"][i], p["ssm_a_re"][j], p["ssm_a_im"][j], p["ssm_b_re"][j],
                      p["ssm_b_im"][j], p["ssm_c_re"][j], p["ssm_c_im"][j], p["ssm_d"][j], p["ssm_log_dt"][j])
        pre, w_pre = "glu", p["ssm_w_glu"][j]
    elif m == 1:
        x = _conv_mixer(x3, p["norm_mix"][i], p["conv_w_pw1"][j], p["conv_b_pw1"][j], p["conv_w_dw"][j],
                        p["conv_b_dw"][j], p["conv_ln_g"][j], p["conv_ln_b"][j], p["conv_w_pw2"][j],
                        p["conv_b_pw2"][j]).reshape(bsz * s, d)
    elif m == 2:
        x = _gmlp_mixer(x, p["norm_mix"][i], p["gmlp_w_in"][j], p["gmlp_ln_g"][j], p["gmlp_ln_b"][j],
                        p["gmlp_w_s"][j], p["gmlp_b_s"][j], p["gmlp_w_out"][j])
    else:
        y = _attention_mixer(x3, p["norm_mix"][i], p["attn_w_qkv"][j])
        pre, w_pre = "proj", p["attn_w_o"][j]
    return _mlp(x, p["norm_mlp"][i], p["mlp_w_in"][i], p["mlp_w_out"][i], p["norm_final"],
                final_norm=final_norm, pre=pre, y2=y, w_pre=w_pre)


def kernel(x, norm_mix, norm_mlp, norm_final, ssm_a_re, ssm_a_im, ssm_b_re, ssm_b_im, ssm_c_re, ssm_c_im, ssm_d, ssm_log_dt, ssm_w_glu, conv_w_pw1, conv_b_pw1, conv_w_dw, conv_b_dw, conv_ln_g, conv_ln_b, conv_w_pw2, conv_b_pw2, gmlp_w_in, gmlp_ln_g, gmlp_ln_b, gmlp_w_s, gmlp_b_s, gmlp_w_out, attn_w_qkv, attn_w_o, mlp_w_in, mlp_w_out):
    p = dict(locals())
    bsz, s, d = x.shape
    depth = norm_mix.shape[0]
    x = x.reshape(bsz * s, d)
    for i in range(depth):
        x = _layer(x, (bsz, s), i, p, final_norm=(i == depth - 1))
    return x.reshape(bsz, s, d)
```

```python
import functools

import jax
import jax.numpy as jnp
from jax import lax
from jax.experimental import pallas as pl
from jax.experimental.pallas import tpu as pltpu

F32 = jnp.float32
BF16 = jnp.bfloat16

EPS = 1e-6
LANES = 128
SUBLANES = 8
VMEM_LIMIT = 56 * 1024 * 1024

SSM_GROUP = 16
SSM_STATE = 64
SSM_CHUNK = 16
GROUPS_PER_SLAB = LANES // SSM_GROUP
CONV_WIDTH = 31
CONV_HALO = 32
CONV_ROWS = 128
CONV_SPLIT = 4
CONV_PITCH = 2
GMLP_CHUNK = 128
GMLP_HEADS = 4
GMLP_SPLIT = 2
ATT_CONFIGS = ((128, 1), (512, 4), (2048, 16))
ATT_HEADS = 8
HEAD_DIM = 64
ATT_WIDTH = ATT_HEADS * HEAD_DIM
ATT_BLOCK = 128
ATT_PAIRS = ATT_WIDTH // LANES
ATT_UNROLL = 32


def _params(n_axes):
    return pltpu.CompilerParams(dimension_semantics=("arbitrary",) * n_axes,
                                vmem_limit_bytes=VMEM_LIMIT)


def _resident(shape):
    zeros = (0,) * len(shape)
    return pl.BlockSpec(shape, lambda *_: zeros, pipeline_mode=pl.Buffered(1))


def _rms(x, g):
    ms = jnp.mean(x * x, axis=-1, keepdims=True)
    return x * lax.rsqrt(ms + EPS) * g


def _layernorm(x, g, b):
    mu = jnp.mean(x, axis=-1, keepdims=True)
    xc = x - mu
    var = jnp.mean(xc * xc, axis=-1, keepdims=True)
    return xc * lax.rsqrt(var + EPS) * g + b


def _dot(a, b):
    return jnp.dot(a, b, preferred_element_type=F32)


def _mlp_kernel(*refs, pre, final_norm):
    if pre is None:
        x_ref, g_ref, win_ref, wout_ref, gf_ref, o_ref = refs
        x = x_ref[...]
    else:
        x_ref, y_ref, wpre_ref, g_ref, win_ref, wout_ref, gf_ref, o_ref = refs
        d = x_ref.shape[1]
        z = _dot(y_ref[...].astype(BF16), wpre_ref[...])
        x = x_ref[...] + (z[:, :d] * jax.nn.sigmoid(z[:, d:]) if pre == "glu" else z)
    h = _rms(x, g_ref[...]).astype(BF16)
    a = jnp.maximum(_dot(h, win_ref[...]), 0.0)
    a = (a * a).astype(BF16)
    y = x + _dot(a, wout_ref[...])
    if final_norm:
        y = _rms(y, gf_ref[...])
    o_ref[...] = y


def _mlp(x2, g, w_in, w_out, g_final, final_norm, pre=None, y2=None, w_pre=None, tm=512):
    t, d = x2.shape
    dff = w_in.shape[1]
    row = pl.BlockSpec((tm, d), lambda i: (i, 0))
    ins, specs = [x2], [row]
    if pre is not None:
        ins += [y2, w_pre.astype(BF16)]
        specs += [pl.BlockSpec((tm, y2.shape[1]), lambda i: (i, 0)), _resident(w_pre.shape)]
    ins += [g.reshape(1, d), w_in.astype(BF16), w_out.astype(BF16), g_final.reshape(1, d)]
    specs += [_resident((1, d)), _resident((d, dff)), _resident((dff, d)), _resident((1, d))]
    return pl.pallas_call(
        functools.partial(_mlp_kernel, pre=pre, final_norm=final_norm),
        grid=(t // tm,),
        in_specs=specs,
        out_specs=row,
        out_shape=jax.ShapeDtypeStruct((t, d), F32),
        compiler_params=_params(1),
        name="mlp" + ("_" + pre if pre else "") + ("_final" if final_norm else ""),
    )(*ins)


def _norm_kernel(x_ref, g_ref, o_ref):
    o_ref[...] = _rms(x_ref[...], g_ref[...])


def _norm(x2, g, tm=1024):
    t, d = x2.shape
    row = pl.BlockSpec((tm, d), lambda i: (i, 0))
    return pl.pallas_call(
        _norm_kernel, grid=(t // tm,), in_specs=[row, _resident((1, d))], out_specs=row,
        out_shape=jax.ShapeDtypeStruct((t, d), F32), compiler_params=_params(1), name="rmsnorm",
    )(x2, g.reshape(1, d))


def _s5_weights(a_re, a_im, b_re, b_im, c_re, c_im, log_dt):
    hp = lax.Precision.HIGHEST
    n_groups, n_state = a_re.shape
    L, gs = SSM_CHUNK, GROUPS_PER_SLAB
    n_slabs = n_groups // gs
    a = lax.complex(a_re, a_im)
    lam = a * jnp.exp(log_dt)[:, None]
    a_bar = jnp.exp(lam)
    b_bar = ((a_bar - 1.0) / a)[..., None] * lax.complex(b_re, b_im)
    c = lax.complex(c_re, c_im)
    steps = jnp.arange(L + 1, dtype=F32)
    pw = jnp.exp(lam[None] * steps[:, None, None])
    pw_rev = jnp.exp(lam[None] * (L - 1 - steps[:L])[:, None, None])

    kern = jnp.real(jnp.einsum('gpn,lgnq->lgpq', c, pw[:L, :, :, None] * b_bar[None], precision=hp))
    kq = kern.transpose(0, 1, 3, 2).reshape(L, n_slabs, gs, SSM_GROUP, SSM_GROUP)
    toe = jnp.einsum('lcgqp,gh->clgqhp', kq, jnp.eye(gs, dtype=F32), precision=hp)
    toe = toe.reshape(n_slabs, L, LANES, LANES)

    inj = pw_rev[:, :, :, None] * b_bar[None]
    inj = jnp.stack([jnp.real(inj), jnp.imag(inj)], axis=0)
    inj = inj.reshape(2, L, n_slabs, gs, n_state, SSM_GROUP).transpose(2, 0, 1, 3, 5, 4)
    inj = inj.reshape(n_slabs, 2, L, LANES, n_state)
    inj = jnp.concatenate([inj, inj], axis=-1)

    ro = c[None] * pw[1:, :, None, :]
    ro = jnp.stack([jnp.real(ro), -jnp.imag(ro)], axis=0)
    ro = ro.reshape(2, L, n_slabs, gs, SSM_GROUP, n_state).transpose(2, 0, 1, 5, 3, 4)
    ro = ro.reshape(n_slabs, 2, L, n_state, LANES)
    ro = jnp.concatenate([ro, ro], axis=-2)

    a_chunk = pw[L].reshape(n_slabs, 1, gs * n_state)
    return (toe.astype(BF16), inj.astype(BF16), ro.astype(BF16), jnp.real(a_chunk), jnp.imag(a_chunk))


def _s5_expand(toe_ref, inj_ref, ro_ref, toe_s, inj_s, ro_s):
    L = SSM_CHUNK
    half = inj_s.shape[1] // 2
    row = lax.broadcasted_iota(jnp.int32, (LANES, LANES), 0)
    lane = lax.broadcasted_iota(jnp.int32, (LANES, LANES), 1)
    zero = jnp.zeros((LANES, LANES), F32)
    blk = lambda k: pl.ds(k * LANES, LANES)
    for j in range(L):
        for i in range(j - (j % 2), L):
            toe_s[blk(j), blk(i)] = toe_ref[i - j] if i >= j else zero.astype(BF16)
    for ri in range(2):
        for gp in range(half // LANES):
            in_pair = (row // SSM_GROUP // 2 == gp) & (row // SSM_GROUP % 2 == lane // SSM_STATE)
            out_pair = lane // SSM_GROUP == 2 * gp + row // SSM_STATE
            for t in range(L):
                inj_s[blk(t), pl.ds(ri * half + gp * LANES, LANES)] = jnp.where(
                    in_pair, inj_ref[ri, t].astype(F32), zero).astype(BF16)
                ro_s[pl.ds(ri * half + gp * LANES, LANES), blk(t)] = jnp.where(
                    out_pair, ro_ref[ri, t].astype(F32), zero).astype(BF16)


def _s5_kernel(u_ref, toe_ref, inj_ref, ro_ref, ar_ref, ai_ref, d_ref, y_ref,
               toe_s, inj_s, ro_s, z_s, x_s):
    L = SSM_CHUNK
    n_chunks = u_ref.shape[0] // L
    half = ar_ref.shape[1]

    @pl.when(pl.program_id(1) == 0)
    def _():
        _s5_expand(toe_ref, inj_ref, ro_ref, toe_s, inj_s, ro_s)

    a_tok = [u_ref[pl.ds(j, n_chunks, stride=L), :] for j in range(L)]
    acat = jnp.concatenate([t.astype(BF16) for t in a_tok], axis=1)
    z_s[...] = _dot(acat, inj_s[...])

    ar, ai = ar_ref[...], ai_ref[...]

    def step(c, carry):
        xr, xi = carry
        x_s[pl.ds(c, 1), :] = jnp.concatenate([xr, xi], axis=1)
        z = z_s[pl.ds(c, 1), :]
        return (ar * xr - ai * xi + z[:, :half], ar * xi + ai * xr + z[:, half:])

    zero = jnp.zeros((1, half), F32)
    lax.fori_loop(0, n_chunks, step, (zero, zero), unroll=SUBLANES)
    x_in = x_s[...].astype(BF16)

    d = d_ref[...]
    pair = 2 * LANES
    for it in range(L // 2):
        k_hi = (it + 1) * pair
        cols = slice(it * pair, (it + 1) * pair)
        o = _dot(acat[:, :k_hi], toe_s[:k_hi, cols]) + _dot(x_in, ro_s[:, cols])
        for s in range(2):
            i = 2 * it + s
            y = o[:, s * LANES:(s + 1) * LANES] + d * a_tok[i]
            y_ref[pl.ds(i, n_chunks, stride=L), :] = jax.nn.gelu(y)


def _s5_scan(u3, weights, d_skip):
    bsz, s, d = u3.shape
    toe, inj, ro, a_r, a_i = weights
    n_slabs = d // LANES
    n_chunks = s // SSM_CHUNK
    n_st = a_r.shape[-1]
    cw = SSM_CHUNK * LANES

    def slab(shape):
        nd = len(shape)
        return pl.BlockSpec((None,) + tuple(shape), lambda c, b: (c,) + (0,) * nd)

    act = pl.BlockSpec((None, s, LANES), lambda c, b: (b, 0, c))
    return pl.pallas_call(
        _s5_kernel,
        grid=(n_slabs, bsz),
        in_specs=[act, slab(toe.shape[1:]), slab(inj.shape[1:]), slab(ro.shape[1:]),
                  slab((1, n_st)), slab((1, n_st)), slab((1, LANES))],
        out_specs=act,
        out_shape=jax.ShapeDtypeStruct((bsz, s, d), F32),
        scratch_shapes=[pltpu.VMEM((cw, cw), BF16), pltpu.VMEM((cw, 2 * n_st), BF16),
                        pltpu.VMEM((2 * n_st, cw), BF16),
                        pltpu.VMEM((n_chunks, 2 * n_st), F32), pltpu.VMEM((n_chunks, 2 * n_st), F32)],
        compiler_params=_params(2),
        name="s5_scan",
    )(u3, toe, inj, ro, a_r, a_i, d_skip.reshape(n_slabs, 1, LANES))


def _s5_mixer(x3, g, a_re, a_im, b_re, b_im, c_re, c_im, d_skip, log_dt):
    bsz, s, d = x3.shape
    u = _norm(x3.reshape(bsz * s, d), g)
    y = _s5_scan(u.reshape(bsz, s, d), _s5_weights(a_re, a_im, b_re, b_im, c_re, c_im, log_dt), d_skip)
    return y.reshape(bsz * s, d)


def _conv_kernel(x_ref, g_ref, w1_ref, b1_ref, wdw_ref, bdw_ref, lng_ref, lnb_ref, w2_ref, b2_ref,
                 o_ref, hist, conv_s):
    ts, d = x_ref.shape
    t = pl.program_id(1)
    n_slabs = d // LANES

    def hrows(r0, n):
        return pl.ds(CONV_PITCH * r0, n, stride=CONV_PITCH)

    @pl.when(t == 0)
    def _():
        for c in range(n_slabs):
            hist[c, hrows(0, CONV_HALO), :] = jnp.zeros((CONV_HALO, LANES), F32)

    @pl.when(t > 0)
    def _():
        for c in range(n_slabs):
            hist[c, hrows(0, CONV_HALO), :] = hist[c, hrows(ts, CONV_HALO), :]

    first = CONV_HALO - (CONV_WIDTH - 1)
    rows = ts // CONV_SPLIT
    xs = [x_ref[pl.ds(sec * rows, rows), :] for sec in range(CONV_SPLIT)]
    zs = [_dot(_rms(x, g_ref[...]).astype(BF16), w1_ref[...]) + b1_ref[...] for x in xs]
    for sec in range(CONV_SPLIT):
        r0 = sec * rows
        gated = zs[sec][:, :d] * jax.nn.sigmoid(zs[sec][:, d:])
        for c in range(n_slabs):
            hist[c, hrows(CONV_HALO + r0, rows), :] = gated[:, c * LANES:(c + 1) * LANES]
        for c in range(n_slabs):
            lanes = pl.ds(c * LANES, LANES)
            for rb in range(rows // CONV_ROWS):
                ro = r0 + rb * CONV_ROWS
                acc = jnp.zeros((CONV_ROWS, LANES), F32)
                for k in range(CONV_WIDTH):
                    acc = acc + wdw_ref[pl.ds(k, 1), lanes] * hist[c, hrows(ro + first + k, CONV_ROWS), :]
                conv_s[pl.ds(ro, CONV_ROWS), lanes] = acc
        y = _layernorm(conv_s[pl.ds(r0, rows), :] + bdw_ref[...], lng_ref[...], lnb_ref[...])
        y = (y * jax.nn.sigmoid(y)).astype(BF16)
        o_ref[pl.ds(r0, rows), :] = xs[sec] + _dot(y, w2_ref[...]) + b2_ref[...]


def _conv_mixer(x3, g, w_pw1, b_pw1, w_dw, b_dw, ln_g, ln_b, w_pw2, b_pw2, ts=512):
    bsz, s, d = x3.shape
    act = pl.BlockSpec((None, ts, d), lambda b, t: (b, t, 0))
    vec = lambda v: v.reshape(1, -1)
    return pl.pallas_call(
        _conv_kernel,
        grid=(bsz, s // ts),
        in_specs=[act, _resident((1, d)), _resident((d, 2 * d)), _resident((1, 2 * d)),
                  _resident((CONV_WIDTH, d)), _resident((1, d)), _resident((1, d)), _resident((1, d)),
                  _resident((d, d)), _resident((1, d))],
        out_specs=act,
        out_shape=jax.ShapeDtypeStruct((bsz, s, d), F32),
        scratch_shapes=[pltpu.VMEM((d // LANES, CONV_PITCH * (ts + CONV_HALO), LANES), F32),
                        pltpu.VMEM((ts, d), F32)],
        compiler_params=_params(2),
        name="conv_mixer",
    )(x3, vec(g), w_pw1.astype(BF16), vec(b_pw1), w_dw, vec(b_dw), vec(ln_g), vec(ln_b),
      w_pw2.astype(BF16), vec(b_pw2))


def _gmlp_kernel(x_ref, g_ref, win_ref, lng_ref, lnb_ref, ws_ref, bs_ref, wout_ref, o_ref):
    tm, d = x_ref.shape
    e = wout_ref.shape[0]
    he = e // GMLP_HEADS
    rows = lax.broadcasted_iota(jnp.int32, (GMLP_CHUNK, GMLP_CHUNK), 0)
    cols = lax.broadcasted_iota(jnp.int32, (GMLP_CHUNK, GMLP_CHUNK), 1)
    ws = [jnp.where(rows >= cols, ws_ref[h], 0.0).astype(BF16) for h in range(GMLP_HEADS)]
    bs = [bs_ref[:, h:h + 1] for h in range(GMLP_HEADS)]
    sec = tm // GMLP_SPLIT
    xs = [x_ref[pl.ds(i * sec, sec), :] for i in range(GMLP_SPLIT)]
    zs = [_dot(_rms(x, g_ref[...]).astype(BF16), win_ref[...]) for x in xs]
    for i in range(GMLP_SPLIT):
        z = jax.nn.gelu(zs[i])
        u = z[:, :e]
        v = _layernorm(z[:, e:], lng_ref[...], lnb_ref[...]).astype(BF16)
        mixed = []
        for ck in range(sec // GMLP_CHUNK):
            r0 = ck * GMLP_CHUNK
            mixed.append(jnp.concatenate(
                [_dot(ws[h], v[r0:r0 + GMLP_CHUNK, h * he:(h + 1) * he]) + bs[h] for h in range(GMLP_HEADS)],
                axis=1))
        gated = (u * jnp.concatenate(mixed, axis=0)).astype(BF16)
        o_ref[pl.ds(i * sec, sec), :] = xs[i] + _dot(gated, wout_ref[...])


def _gmlp_mixer(x2, g, w_in, ln_g, ln_b, w_s, b_s, w_out, tm=512):
    t, d = x2.shape
    e = w_out.shape[0]
    row = pl.BlockSpec((tm, d), lambda i: (i, 0))
    return pl.pallas_call(
        _gmlp_kernel,
        grid=(t // tm,),
        in_specs=[row, _resident((1, d)), _resident((d, 2 * e)), _resident((1, e)), _resident((1, e)),
                  _resident(w_s.shape), _resident((GMLP_CHUNK, GMLP_HEADS)), _resident((e, d))],
        out_specs=row,
        out_shape=jax.ShapeDtypeStruct((t, d), F32),
        compiler_params=_params(1),
        name="gmlp_mixer",
    )(x2, g.reshape(1, d), w_in.astype(BF16), ln_g.reshape(1, e), ln_b.reshape(1, e), w_s, b_s.T,
      w_out.astype(BF16))


def _qkv_kernel(x_ref, g_ref, w_ref, *refs):
    outs, xs = refs[:-1], refs[-1]
    tm, d = x_ref.shape
    xn = _rms(x_ref[...], g_ref[...])
    gw = 3 * ATT_WIDTH

    def project(h, gi):
        z = _dot(h.astype(BF16), w_ref[:, gi * gw:(gi + 1) * gw])
        return jnp.concatenate([z[:, :ATT_WIDTH] * (HEAD_DIM ** -0.5), z[:, ATT_WIDTH:]], axis=1).astype(BF16)

    for c in range(d // LANES):
        xs[c] = xn[:, c * LANES:(c + 1) * LANES]
    for gi, (_, dil) in sorted(enumerate(ATT_CONFIGS), key=lambda e: e[1][1]):
        if dil == 1:
            outs[gi][0] = project(xn, gi)
            continue
        sub = tm // dil
        h = jnp.concatenate([jnp.concatenate([xs[c, pl.ds(r, sub, stride=dil), :] for c in range(d // LANES)],
                                             axis=1) for r in range(dil)], axis=0)
        z = project(h, gi)
        for r in range(dil):
            outs[gi][r] = z[r * sub:(r + 1) * sub]


def _qkv(x3, g, w_qkv, tm=512):
    bsz, s, d = x3.shape
    n_groups = len(ATT_CONFIGS)
    gw = 3 * ATT_WIDTH
    w = w_qkv.astype(BF16).reshape(d, 3, n_groups, ATT_WIDTH).transpose(0, 2, 1, 3).reshape(d, n_groups * gw)
    outs = [jax.ShapeDtypeStruct((bsz, dil, s // dil, gw), BF16) for _, dil in ATT_CONFIGS]
    ospecs = [pl.BlockSpec((None, dil, tm // dil, gw), lambda b, i: (b, 0, i, 0)) for _, dil in ATT_CONFIGS]
    return pl.pallas_call(
        _qkv_kernel,
        grid=(bsz, s // tm),
        in_specs=[pl.BlockSpec((None, tm, d), lambda b, i: (b, i, 0)), _resident((1, d)), _resident(w.shape)],
        out_specs=ospecs,
        out_shape=outs,
        scratch_shapes=[pltpu.VMEM((d // LANES, tm, LANES), F32)],
        compiler_params=_params(2),
        name="attn_qkv",
    )(x3, g.reshape(1, d), w)


def _att_group(q_ref, k_ref, v_ref, bias_ref, og, lg, dil):
    blk = ATT_BLOCK
    s = q_ref.shape[0]
    nb = s // (dil * blk)

    low = lax.broadcasted_iota(jnp.int32, (1, LANES), 1) < HEAD_DIM

    def body(idx, carry):
        r, n = idx // nb, idx % nb
        cur = pl.ds(pl.multiple_of(idx * blk, blk), blk)
        prev = pl.ds(pl.multiple_of((idx - jnp.minimum(n, 1)) * blk, blk), blk)
        out_rows = pl.ds(r + n * (blk * dil), blk, stride=dil) if dil > 1 else cur
        q = q_ref[cur, :]
        k = jnp.concatenate([k_ref[prev, :], k_ref[cur, :]], axis=0)
        v = jnp.concatenate([v_ref[prev, :], v_ref[cur, :]], axis=0)
        bias = bias_ref[jnp.minimum(n, 1)]
        qq = jnp.concatenate([jnp.where(low, q, jnp.zeros_like(q)), jnp.where(low, jnp.zeros_like(q), q)], axis=0)
        sc = lax.dot_general(qq, k, (((1,), (1,)), ((), ())), preferred_element_type=F32) + bias
        m = jnp.max(sc, axis=-1, keepdims=True)
        p = jnp.exp(sc - m).astype(BF16)
        pv = _dot(p, jnp.concatenate([v, jnp.ones_like(v)], axis=1))
        num = jnp.where(low, pv[:blk, :LANES], pv[blk:, :LANES])
        den = jnp.where(low, pv[:blk, LANES:], pv[blk:, LANES:])
        og[out_rows, :] = num / den
        lg[out_rows, :] = jnp.where(low, m[:blk], m[blk:]) + jnp.log(den)
        return carry

    lax.fori_loop(0, dil * nb, body, 0, unroll=ATT_UNROLL)


def _att_kernel(*refs):
    n_groups = len(ATT_CONFIGS)
    qkv, bias_ref, o_ref = refs[:3 * n_groups], refs[3 * n_groups], refs[3 * n_groups + 1]
    ogs = refs[3 * n_groups + 2:4 * n_groups + 2]
    lgs = refs[4 * n_groups + 2:]
    for gi, (_, dil) in enumerate(ATT_CONFIGS):
        _att_group(qkv[3 * gi], qkv[3 * gi + 1], qkv[3 * gi + 2], bias_ref, ogs[gi], lgs[gi], dil)

    tile = 512

    def merge(t, carry):
        rs = pl.ds(pl.multiple_of(t * tile, tile), tile)
        ls = [lg[rs, :] for lg in lgs]
        m = functools.reduce(jnp.maximum, ls)
        es = [jnp.exp(l - m) for l in ls]
        merged = sum(e * og[rs, :] for e, og in zip(es, ogs)) / sum(es)
        o_ref[rs, :] = merged.astype(BF16)
        return carry

    lax.fori_loop(0, o_ref.shape[0] // tile, merge, 0)


def _attention(qkv_groups):
    bsz, _, _, gw = qkv_groups[0].shape
    s = qkv_groups[0].shape[1] * qkv_groups[0].shape[2]
    per = ATT_WIDTH // LANES
    ins, specs = [], []
    for a in qkv_groups:
        for which in range(3):
            ins.append(a.reshape(bsz, s, gw))
            specs.append(pl.BlockSpec((None, s, LANES), lambda b, p, which=which: (b, 0, which * per + p)))
    i_idx = lax.broadcasted_iota(jnp.int32, (ATT_BLOCK, 2 * ATT_BLOCK), 0)
    j_idx = lax.broadcasted_iota(jnp.int32, (ATT_BLOCK, 2 * ATT_BLOCK), 1)
    dist = i_idx + ATT_BLOCK - j_idx
    band = (dist >= 0) & (dist <= ATT_BLOCK)
    bias = jnp.where(jnp.stack([band & (j_idx >= ATT_BLOCK), band]), 0.0, -jnp.inf).astype(F32)
    bias = jnp.concatenate([bias, bias], axis=1)
    buf = pltpu.VMEM((s, LANES), F32)
    return pl.pallas_call(
        _att_kernel,
        grid=(bsz, ATT_PAIRS),
        in_specs=specs + [_resident(bias.shape)],
        out_specs=pl.BlockSpec((None, s, LANES), lambda b, p: (b, 0, p)),
        out_shape=jax.ShapeDtypeStruct((bsz, s, ATT_WIDTH), BF16),
        scratch_shapes=[buf] * (2 * len(ATT_CONFIGS)),
        compiler_params=_params(2),
        name="attn_core",
    )(*ins, bias)


def _attention_mixer(x3, g, w_qkv):
    bsz, s, d = x3.shape
    for window, dil in ATT_CONFIGS:
        assert window // dil == ATT_BLOCK and s % (dil * ATT_BLOCK) == 0
    return _attention(_qkv(x3, g, w_qkv)).reshape(bsz * s, ATT_WIDTH)


def _layer(x, seq_shape, i, p, final_norm):
    bsz, s = seq_shape
    d = x.shape[1]
    m, j = i % 4, i // 4
    x3 = x.reshape(bsz, s, d)
    pre, y, w_pre = None, None, None
    if m == 0:
        y = _s5_mixer(x3, p["norm_mix"][i], p["ssm_a_re"][j], p["ssm_a_im"][j], p["ssm_b_re"][j],
                      p["ssm_b_im"][j], p["ssm_c_re"][j], p["ssm_c_im"][j], p["ssm_d"][j], p["ssm_log_dt"][j])
        pre, w_pre = "glu", p["ssm_w_glu"][j]
    elif m == 1:
        x = _conv_mixer(x3, p["norm_mix"][i], p["conv_w_pw1"][j], p["conv_b_pw1"][j], p["conv_w_dw"][j],
                        p["conv_b_dw"][j], p["conv_ln_g"][j], p["conv_ln_b"][j], p["conv_w_pw2"][j],
                        p["conv_b_pw2"][j]).reshape(bsz * s, d)
    elif m == 2:
        x = _gmlp_mixer(x, p["norm_mix"][i], p["gmlp_w_in"][j], p["gmlp_ln_g"][j], p["gmlp_ln_b"][j],
                        p["gmlp_w_s"][j], p["gmlp_b_s"][j], p["gmlp_w_out"][j])
    else:
        y = _attention_mixer(x3, p["norm_mix"][i], p["attn_w_qkv"][j])
        pre, w_pre = "proj", p["attn_w_o"][j]
    return _mlp(x, p["norm_mlp"][i], p["mlp_w_in"][i], p["mlp_w_out"][i], p["norm_final"],
                final_norm=final_norm, pre=pre, y2=y, w_pre=w_pre)


def kernel(x, norm_mix, norm_mlp, norm_final, ssm_a_re, ssm_a_im, ssm_b_re, ssm_b_im, ssm_c_re, ssm_c_im, ssm_d, ssm_log_dt, ssm_w_glu, conv_w_pw1, conv_b_pw1, conv_w_dw, conv_b_dw, conv_ln_g, conv_ln_b, conv_w_pw2, conv_b_pw2, gmlp_w_in, gmlp_ln_g, gmlp_ln_b, gmlp_w_s, gmlp_b_s, gmlp_w_out, attn_w_qkv, attn_w_o, mlp_w_in, mlp_w_out):
    p = dict(locals())
    bsz, s, d = x.shape
    depth = norm_mix.shape[0]
    x = x.reshape(bsz * s, d)
    for i in range(depth):
        x = _layer(x, (bsz, s), i, p, final_norm=(i == depth - 1))
    return x.reshape(bsz, s, d)
```

```python
import functools

import jax
import jax.numpy as jnp
from jax import lax
from jax.experimental import pallas as pl
from jax.experimental.pallas import tpu as pltpu

F32 = jnp.float32
BF16 = jnp.bfloat16

EPS = 1e-6
LANES = 128
SUBLANES = 8
VMEM_LIMIT = 56 * 1024 * 1024

SSM_GROUP = 16
SSM_STATE = 64
SSM_CHUNK = 16
SSM_BATCH = 2
SSM_PITCH = 264
GROUPS_PER_SLAB = LANES // SSM_GROUP
CONV_WIDTH = 31
CONV_HALO = 32
CONV_ROWS = 128
CONV_SPLIT = 1
CONV_PITCH = 2
GMLP_CHUNK = 128
GMLP_HEADS = 4
GMLP_SPLIT = 2
ATT_CONFIGS = ((128, 1), (512, 4), (2048, 16))
ATT_HEADS = 8
HEAD_DIM = 64
ATT_WIDTH = ATT_HEADS * HEAD_DIM
ATT_BLOCK = 128
ATT_PAIRS = ATT_WIDTH // LANES
ATT_UNROLL = 32


def _params(n_axes):
    return pltpu.CompilerParams(dimension_semantics=("arbitrary",) * n_axes,
                                vmem_limit_bytes=VMEM_LIMIT)


def _resident(shape):
    zeros = (0,) * len(shape)
    return pl.BlockSpec(shape, lambda *_: zeros, pipeline_mode=pl.Buffered(1))


def _rms(x, g):
    ms = jnp.mean(x * x, axis=-1, keepdims=True)
    return x * lax.rsqrt(ms + EPS) * g


def _layernorm(x, g, b):
    mu = jnp.mean(x, axis=-1, keepdims=True)
    xc = x - mu
    var = jnp.mean(xc * xc, axis=-1, keepdims=True)
    return xc * lax.rsqrt(var + EPS) * g + b


def _dot(a, b):
    return jnp.dot(a, b, preferred_element_type=F32)


def _mlp_kernel(*refs, pre, final_norm):
    if pre is None:
        x_ref, g_ref, win_ref, wout_ref, gf_ref, o_ref = refs
        x = x_ref[...]
    else:
        x_ref, y_ref, wpre_ref, g_ref, win_ref, wout_ref, gf_ref, o_ref = refs
        d = x_ref.shape[1]
        z = _dot(y_ref[...].astype(BF16), wpre_ref[...])
        x = x_ref[...] + (z[:, :d] * jax.nn.sigmoid(z[:, d:]) if pre == "glu" else z)
    h = _rms(x, g_ref[...]).astype(BF16)
    a = jnp.maximum(_dot(h, win_ref[...]), 0.0)
    a = (a * a).astype(BF16)
    y = x + _dot(a, wout_ref[...])
    if final_norm:
        y = _rms(y, gf_ref[...])
    o_ref[...] = y


def _mlp(x2, g, w_in, w_out, g_final, final_norm, pre=None, y2=None, w_pre=None, tm=512):
    t, d = x2.shape
    dff = w_in.shape[1]
    row = pl.BlockSpec((tm, d), lambda i: (i, 0))
    ins, specs = [x2], [row]
    if pre is not None:
        ins += [y2, w_pre.astype(BF16)]
        specs += [pl.BlockSpec((tm, y2.shape[1]), lambda i: (i, 0)), _resident(w_pre.shape)]
    ins += [g.reshape(1, d), w_in.astype(BF16), w_out.astype(BF16), g_final.reshape(1, d)]
    specs += [_resident((1, d)), _resident((d, dff)), _resident((dff, d)), _resident((1, d))]
    return pl.pallas_call(
        functools.partial(_mlp_kernel, pre=pre, final_norm=final_norm),
        grid=(t // tm,),
        in_specs=specs,
        out_specs=row,
        out_shape=jax.ShapeDtypeStruct((t, d), F32),
        compiler_params=_params(1),
        name="mlp" + ("_" + pre if pre else "") + ("_final" if final_norm else ""),
    )(*ins)


def _norm_kernel(x_ref, g_ref, o_ref):
    o_ref[...] = _rms(x_ref[...], g_ref[...])


def _norm(x2, g, tm=1024):
    t, d = x2.shape
    row = pl.BlockSpec((tm, d), lambda i: (i, 0))
    return pl.pallas_call(
        _norm_kernel, grid=(t // tm,), in_specs=[row, _resident((1, d))], out_specs=row,
        out_shape=jax.ShapeDtypeStruct((t, d), F32), compiler_params=_params(1), name="rmsnorm",
    )(x2, g.reshape(1, d))


def _s5_weights(a_re, a_im, b_re, b_im, c_re, c_im, log_dt):
    hp = lax.Precision.HIGHEST
    n_groups, n_state = a_re.shape
    L, gs = SSM_CHUNK, GROUPS_PER_SLAB
    n_slabs = n_groups // gs
    a = lax.complex(a_re, a_im)
    lam = a * jnp.exp(log_dt)[:, None]
    a_bar = jnp.exp(lam)
    b_bar = ((a_bar - 1.0) / a)[..., None] * lax.complex(b_re, b_im)
    c = lax.complex(c_re, c_im)
    steps = jnp.arange(L + 1, dtype=F32)
    pw = jnp.exp(lam[None] * steps[:, None, None])
    pw_rev = jnp.exp(lam[None] * (L - 1 - steps[:L])[:, None, None])

    kern = jnp.real(jnp.einsum('gpn,lgnq->lgpq', c, pw[:L, :, :, None] * b_bar[None], precision=hp))
    kq = kern.transpose(0, 1, 3, 2).reshape(L, n_slabs, gs, SSM_GROUP, SSM_GROUP)
    toe = jnp.einsum('lcgqp,gh->clgqhp', kq, jnp.eye(gs, dtype=F32), precision=hp)
    toe = toe.reshape(n_slabs, L, LANES, LANES)

    inj = pw_rev[:, :, :, None] * b_bar[None]
    inj = jnp.stack([jnp.real(inj), jnp.imag(inj)], axis=0)
    inj = inj.reshape(2, L, n_slabs, gs, n_state, SSM_GROUP).transpose(2, 0, 1, 3, 5, 4)
    inj = inj.reshape(n_slabs, 2, L, LANES, n_state)
    inj = jnp.concatenate([inj, inj], axis=-1)

    ro = c[None] * pw[1:, :, None, :]
    ro = jnp.stack([jnp.real(ro), -jnp.imag(ro)], axis=0)
    ro = ro.reshape(2, L, n_slabs, gs, SSM_GROUP, n_state).transpose(2, 0, 1, 5, 3, 4)
    ro = ro.reshape(n_slabs, 2, L, n_state, LANES)
    ro = jnp.concatenate([ro, ro], axis=-2)

    a_chunk = pw[L].reshape(n_slabs, gs * n_state // LANES, LANES)
    a1 = jnp.concatenate([jnp.real(a_chunk), jnp.real(a_chunk)], axis=1)
    a2 = jnp.concatenate([-jnp.imag(a_chunk), jnp.imag(a_chunk)], axis=1)
    return toe.astype(BF16), inj.astype(BF16), ro.astype(BF16), a1, a2


def _s5_expand(toe_ref, inj_ref, ro_ref, toe_s, inj_s, ro_s):
    L = SSM_CHUNK
    half = inj_s.shape[1] // 2
    row = lax.broadcasted_iota(jnp.int32, (LANES, LANES), 0)
    lane = lax.broadcasted_iota(jnp.int32, (LANES, LANES), 1)
    zero = jnp.zeros((LANES, LANES), F32)
    blk = lambda k: pl.ds(k * LANES, LANES)
    for j in range(L):
        for i in range(j - (j % 2), L):
            toe_s[blk(j), blk(i)] = toe_ref[i - j] if i >= j else zero.astype(BF16)
    for ri in range(2):
        for gp in range(half // LANES):
            in_pair = (row // SSM_GROUP // 2 == gp) & (row // SSM_GROUP % 2 == lane // SSM_STATE)
            out_pair = lane // SSM_GROUP == 2 * gp + row // SSM_STATE
            for t in range(L):
                inj_s[blk(t), pl.ds(ri * half + gp * LANES, LANES)] = jnp.where(
                    in_pair, inj_ref[ri, t].astype(F32), zero).astype(BF16)
                ro_s[pl.ds(ri * half + gp * LANES, LANES), blk(t)] = jnp.where(
                    out_pair, ro_ref[ri, t].astype(F32), zero).astype(BF16)


def _s5_kernel(u_ref, toe_ref, inj_ref, ro_ref, a1_ref, a2_ref, d_ref, y_ref,
               toe_s, inj_s, ro_s, z_s, x_s):
    L = SSM_CHUNK
    nbt = u_ref.shape[0]
    n_chunks = u_ref.shape[1] // L
    n_blk = inj_s.shape[1] // LANES

    @pl.when(pl.program_id(1) == 0)
    def _():
        _s5_expand(toe_ref, inj_ref, ro_ref, toe_s, inj_s, ro_s)

    a_tok = [jnp.concatenate([u_ref[b, pl.ds(j, n_chunks, stride=L), :] for b in range(nbt)], axis=0)
             for j in range(L)]
    acat = jnp.concatenate([t.astype(BF16) for t in a_tok], axis=1)
    z = _dot(acat, inj_s[...])
    for b in range(nbt):
        for k in range(n_blk):
            z_s[b, pl.ds(k * SSM_PITCH, n_chunks), :] = z[b * n_chunks:(b + 1) * n_chunks, k * LANES:(k + 1) * LANES]

    a1, a2 = a1_ref[...], a2_ref[...]

    def step(c, xs):
        rows = pl.ds(c, n_blk, stride=SSM_PITCH)
        out = []
        for b, x in enumerate(xs):
            x_s[b, rows, :] = x
            out.append(a1 * x + a2 * pltpu.roll(x, n_blk // 2, axis=0) + z_s[b, rows, :])
        return tuple(out)

    lax.fori_loop(0, n_chunks, step, (jnp.zeros((n_blk, LANES), F32),) * nbt, unroll=SUBLANES)
    x_in = jnp.concatenate(
        [jnp.concatenate([x_s[b, pl.ds(k * SSM_PITCH, n_chunks), :] for k in range(n_blk)], axis=1)
         for b in range(nbt)], axis=0).astype(BF16)

    d = d_ref[...]
    pair = 2 * LANES
    for it in range(L // 2):
        k_hi = (it + 1) * pair
        cols = slice(it * pair, (it + 1) * pair)
        o = _dot(acat[:, :k_hi], toe_s[:k_hi, cols]) + _dot(x_in, ro_s[:, cols])
        for s in range(2):
            i = 2 * it + s
            y = jax.nn.gelu(o[:, s * LANES:(s + 1) * LANES] + d * a_tok[i])
            for b in range(nbt):
                y_ref[b, pl.ds(i, n_chunks, stride=L), :] = y[b * n_chunks:(b + 1) * n_chunks]


def _s5_scan(u3, weights, d_skip):
    bsz, s, d = u3.shape
    toe, inj, ro, a1, a2 = weights
    n_slabs = d // LANES
    n_blk = a1.shape[1]
    cw = SSM_CHUNK * LANES
    assert bsz % SSM_BATCH == 0 and s // SSM_CHUNK <= SSM_PITCH

    def slab(shape):
        nd = len(shape)
        return pl.BlockSpec((None,) + tuple(shape), lambda c, b: (c,) + (0,) * nd)

    act = pl.BlockSpec((SSM_BATCH, s, LANES), lambda c, b: (b, 0, c))
    scan_buf = pltpu.VMEM((SSM_BATCH, n_blk * SSM_PITCH, LANES), F32)
    return pl.pallas_call(
        _s5_kernel,
        grid=(n_slabs, bsz // SSM_BATCH),
        in_specs=[act, slab(toe.shape[1:]), slab(inj.shape[1:]), slab(ro.shape[1:]),
                  slab(a1.shape[1:]), slab(a2.shape[1:]), slab((1, LANES))],
        out_specs=act,
        out_shape=jax.ShapeDtypeStruct((bsz, s, d), F32),
        scratch_shapes=[pltpu.VMEM((cw, cw), BF16), pltpu.VMEM((cw, n_blk * LANES), BF16),
                        pltpu.VMEM((n_blk * LANES, cw), BF16), scan_buf, scan_buf],
        compiler_params=_params(2),
        name="s5_scan",
    )(u3, toe, inj, ro, a1, a2, d_skip.reshape(n_slabs, 1, LANES))


def _s5_mixer(x3, g, a_re, a_im, b_re, b_im, c_re, c_im, d_skip, log_dt):
    bsz, s, d = x3.shape
    u = _norm(x3.reshape(bsz * s, d), g)
    y = _s5_scan(u.reshape(bsz, s, d), _s5_weights(a_re, a_im, b_re, b_im, c_re, c_im, log_dt), d_skip)
    return y.reshape(bsz * s, d)


def _conv_kernel(x_ref, g_ref, w1_ref, b1_ref, wdw_ref, bdw_ref, lng_ref, lnb_ref, w2_ref, b2_ref,
                 o_ref, hist, conv_s):
    ts, d = x_ref.shape
    t = pl.program_id(1)
    n_slabs = d // LANES

    def hrows(r0, n):
        return pl.ds(CONV_PITCH * r0, n, stride=CONV_PITCH)

    @pl.when(t == 0)
    def _():
        for c in range(n_slabs):
            hist[c, hrows(0, CONV_HALO), :] = jnp.zeros((CONV_HALO, LANES), F32)

    @pl.when(t > 0)
    def _():
        for c in range(n_slabs):
            hist[c, hrows(0, CONV_HALO), :] = hist[c, hrows(ts, CONV_HALO), :]

    first = CONV_HALO - (CONV_WIDTH - 1)
    rows = ts // CONV_SPLIT
    xs = [x_ref[pl.ds(sec * rows, rows), :] for sec in range(CONV_SPLIT)]
    zs = [_dot(_rms(x, g_ref[...]).astype(BF16), w1_ref[...]) + b1_ref[...] for x in xs]
    for sec in range(CONV_SPLIT):
        r0 = sec * rows
        gated = zs[sec][:, :d] * jax.nn.sigmoid(zs[sec][:, d:])
        for c in range(n_slabs):
            hist[c, hrows(CONV_HALO + r0, rows), :] = gated[:, c * LANES:(c + 1) * LANES]
        for c in range(n_slabs):
            lanes = pl.ds(c * LANES, LANES)
            for rb in range(rows // CONV_ROWS):
                ro = r0 + rb * CONV_ROWS
                acc = jnp.zeros((CONV_ROWS, LANES), F32)
                for k in range(CONV_WIDTH):
                    acc = acc + wdw_ref[pl.ds(k, 1), lanes] * hist[c, hrows(ro + first + k, CONV_ROWS), :]
                conv_s[pl.ds(ro, CONV_ROWS), lanes] = acc
        y = _layernorm(conv_s[pl.ds(r0, rows), :] + bdw_ref[...], lng_ref[...], lnb_ref[...])
        y = (y * jax.nn.sigmoid(y)).astype(BF16)
        o_ref[pl.ds(r0, rows), :] = xs[sec] + _dot(y, w2_ref[...]) + b2_ref[...]


def _conv_mixer(x3, g, w_pw1, b_pw1, w_dw, b_dw, ln_g, ln_b, w_pw2, b_pw2, ts=512):
    bsz, s, d = x3.shape
    act = pl.BlockSpec((None, ts, d), lambda b, t: (b, t, 0))
    vec = lambda v: v.reshape(1, -1)
    return pl.pallas_call(
        _conv_kernel,
        grid=(bsz, s // ts),
        in_specs=[act, _resident((1, d)), _resident((d, 2 * d)), _resident((1, 2 * d)),
                  _resident((CONV_WIDTH, d)), _resident((1, d)), _resident((1, d)), _resident((1, d)),
                  _resident((d, d)), _resident((1, d))],
        out_specs=act,
        out_shape=jax.ShapeDtypeStruct((bsz, s, d), F32),
        scratch_shapes=[pltpu.VMEM((d // LANES, CONV_PITCH * (ts + CONV_HALO), LANES), F32),
                        pltpu.VMEM((ts, d), F32)],
        compiler_params=_params(2),
        name="conv_mixer",
    )(x3, vec(g), w_pw1.astype(BF16), vec(b_pw1), w_dw, vec(b_dw), vec(ln_g), vec(ln_b),
      w_pw2.astype(BF16), vec(b_pw2))


def _gmlp_kernel(x_ref, g_ref, win_ref, lng_ref, lnb_ref, ws_ref, bs_ref, wout_ref, o_ref):
    tm, d = x_ref.shape
    e = wout_ref.shape[0]
    he = e // GMLP_HEADS
    rows = lax.broadcasted_iota(jnp.int32, (GMLP_CHUNK, GMLP_CHUNK), 0)
    cols = lax.broadcasted_iota(jnp.int32, (GMLP_CHUNK, GMLP_CHUNK), 1)
    ws = [jnp.where(rows >= cols, ws_ref[h], 0.0).astype(BF16) for h in range(GMLP_HEADS)]
    bs = [bs_ref[:, h:h + 1] for h in range(GMLP_HEADS)]
    sec = tm // GMLP_SPLIT
    xs = [x_ref[pl.ds(i * sec, sec), :] for i in range(GMLP_SPLIT)]
    zs = [_dot(_rms(x, g_ref[...]).astype(BF16), win_ref[...]) for x in xs]
    for i in range(GMLP_SPLIT):
        z = jax.nn.gelu(zs[i])
        u = z[:, :e]
        v = _layernorm(z[:, e:], lng_ref[...], lnb_ref[...]).astype(BF16)
        mixed = []
        for ck in range(sec // GMLP_CHUNK):
            r0 = ck * GMLP_CHUNK
            mixed.append(jnp.concatenate(
                [_dot(ws[h], v[r0:r0 + GMLP_CHUNK, h * he:(h + 1) * he]) + bs[h] for h in range(GMLP_HEADS)],
                axis=1))
        gated = (u * jnp.concatenate(mixed, axis=0)).astype(BF16)
        o_ref[pl.ds(i * sec, sec), :] = xs[i] + _dot(gated, wout_ref[...])


def _gmlp_mixer(x2, g, w_in, ln_g, ln_b, w_s, b_s, w_out, tm=512):
    t, d = x2.shape
    e = w_out.shape[0]
    row = pl.BlockSpec((tm, d), lambda i: (i, 0))
    return pl.pallas_call(
        _gmlp_kernel,
        grid=(t // tm,),
        in_specs=[row, _resident((1, d)), _resident((d, 2 * e)), _resident((1, e)), _resident((1, e)),
                  _resident(w_s.shape), _resident((GMLP_CHUNK, GMLP_HEADS)), _resident((e, d))],
        out_specs=row,
        out_shape=jax.ShapeDtypeStruct((t, d), F32),
        compiler_params=_params(1),
        name="gmlp_mixer",
    )(x2, g.reshape(1, d), w_in.astype(BF16), ln_g.reshape(1, e), ln_b.reshape(1, e), w_s, b_s.T,
      w_out.astype(BF16))


def _qkv_kernel(x_ref, g_ref, w_ref, *refs):
    outs, xs = refs[:-1], refs[-1]
    tm, d = x_ref.shape
    xn = _rms(x_ref[...], g_ref[...])
    gw = 3 * ATT_WIDTH

    def project(h, gi):
        z = _dot(h.astype(BF16), w_ref[:, gi * gw:(gi + 1) * gw])
        return jnp.concatenate([z[:, :ATT_WIDTH] * (HEAD_DIM ** -0.5), z[:, ATT_WIDTH:]], axis=1).astype(BF16)

    for c in range(d // LANES):
        xs[c] = xn[:, c * LANES:(c + 1) * LANES]
    for gi, (_, dil) in sorted(enumerate(ATT_CONFIGS), key=lambda e: e[1][1]):
        if dil == 1:
            outs[gi][0] = project(xn, gi)
            continue
        sub = tm // dil
        h = jnp.concatenate([jnp.concatenate([xs[c, pl.ds(r, sub, stride=dil), :] for c in range(d // LANES)],
                                             axis=1) for r in range(dil)], axis=0)
        z = project(h, gi)
        for r in range(dil):
            outs[gi][r] = z[r * sub:(r + 1) * sub]


def _qkv(x3, g, w_qkv, tm=512):
    bsz, s, d = x3.shape
    n_groups = len(ATT_CONFIGS)
    gw = 3 * ATT_WIDTH
    w = w_qkv.astype(BF16).reshape(d, 3, n_groups, ATT_WIDTH).transpose(0, 2, 1, 3).reshape(d, n_groups * gw)
    outs = [jax.ShapeDtypeStruct((bsz, dil, s // dil, gw), BF16) for _, dil in ATT_CONFIGS]
    ospecs = [pl.BlockSpec((None, dil, tm // dil, gw), lambda b, i: (b, 0, i, 0)) for _, dil in ATT_CONFIGS]
    return pl.pallas_call(
        _qkv_kernel,
        grid=(bsz, s // tm),
        in_specs=[pl.BlockSpec((None, tm, d), lambda b, i: (b, i, 0)), _resident((1, d)), _resident(w.shape)],
        out_specs=ospecs,
        out_shape=outs,
        scratch_shapes=[pltpu.VMEM((d // LANES, tm, LANES), F32)],
        compiler_params=_params(2),
        name="attn_qkv",
    )(x3, g.reshape(1, d), w)


def _att_group(q_ref, k_ref, v_ref, bias_ref, og, lg, dil):
    blk = ATT_BLOCK
    s = q_ref.shape[0]
    nb = s // (dil * blk)

    low = lax.broadcasted_iota(jnp.int32, (1, LANES), 1) < HEAD_DIM

    def body(idx, carry):
        r, n = idx // nb, idx % nb
        cur = pl.ds(pl.multiple_of(idx * blk, blk), blk)
        prev = pl.ds(pl.multiple_of((idx - jnp.minimum(n, 1)) * blk, blk), blk)
        out_rows = pl.ds(r + n * (blk * dil), blk, stride=dil) if dil > 1 else cur
        q = q_ref[cur, :]
        k = jnp.concatenate([k_ref[prev, :], k_ref[cur, :]], axis=0)
        v = jnp.concatenate([v_ref[prev, :], v_ref[cur, :]], axis=0)
        bias = bias_ref[jnp.minimum(n, 1)]
        qq = jnp.concatenate([jnp.where(low, q, jnp.zeros_like(q)), jnp.where(low, jnp.zeros_like(q), q)], axis=0)
        sc = lax.dot_general(qq, k, (((1,), (1,)), ((), ())), preferred_element_type=F32) + bias
        m = jnp.max(sc, axis=-1, keepdims=True)
        p = jnp.exp(sc - m).astype(BF16)
        pv = _dot(p, jnp.concatenate([v, jnp.ones_like(v)], axis=1))
        num = jnp.where(low, pv[:blk, :LANES], pv[blk:, :LANES])
        den = jnp.where(low, pv[:blk, LANES:], pv[blk:, LANES:])
        og[out_rows, :] = num / den
        lg[out_rows, :] = jnp.where(low, m[:blk], m[blk:]) + jnp.log(den)
        return carry

    lax.fori_loop(0, dil * nb, body, 0, unroll=ATT_UNROLL)


def _att_kernel(*refs):
    n_groups = len(ATT_CONFIGS)
    qkv, bias_ref, o_ref = refs[:3 * n_groups], refs[3 * n_groups], refs[3 * n_groups + 1]
    ogs = refs[3 * n_groups + 2:4 * n_groups + 2]
    lgs = refs[4 * n_groups + 2:]
    for gi, (_, dil) in enumerate(ATT_CONFIGS):
        _att_group(qkv[3 * gi], qkv[3 * gi + 1], qkv[3 * gi + 2], bias_ref, ogs[gi], lgs[gi], dil)

    tile = 512

    def merge(t, carry):
        rs = pl.ds(pl.multiple_of(t * tile, tile), tile)
        ls = [lg[rs, :] for lg in lgs]
        m = functools.reduce(jnp.maximum, ls)
        es = [jnp.exp(l - m) for l in ls]
        merged = sum(e * og[rs, :] for e, og in zip(es, ogs)) / sum(es)
        o_ref[rs, :] = merged.astype(BF16)
        return carry

    lax.fori_loop(0, o_ref.shape[0] // tile, merge, 0)


def _attention(qkv_groups):
    bsz, _, _, gw = qkv_groups[0].shape
    s = qkv_groups[0].shape[1] * qkv_groups[0].shape[2]
    per = ATT_WIDTH // LANES
    ins, specs = [], []
    for a in qkv_groups:
        for which in range(3):
            ins.append(a.reshape(bsz, s, gw))
            specs.append(pl.BlockSpec((None, s, LANES), lambda b, p, which=which: (b, 0, which * per + p)))
    i_idx = lax.broadcasted_iota(jnp.int32, (ATT_BLOCK, 2 * ATT_BLOCK), 0)
    j_idx = lax.broadcasted_iota(jnp.int32, (ATT_BLOCK, 2 * ATT_BLOCK), 1)
    dist = i_idx + ATT_BLOCK - j_idx
    band = (dist >= 0) & (dist <= ATT_BLOCK)
    bias = jnp.where(jnp.stack([band & (j_idx >= ATT_BLOCK), band]), 0.0, -jnp.inf).astype(F32)
    bias = jnp.concatenate([bias, bias], axis=1)
    buf = pltpu.VMEM((s, LANES), F32)
    return pl.pallas_call(
        _att_kernel,
        grid=(bsz, ATT_PAIRS),
        in_specs=specs + [_resident(bias.shape)],
        out_specs=pl.BlockSpec((None, s, LANES), lambda b, p: (b, 0, p)),
        out_shape=jax.ShapeDtypeStruct((bsz, s, ATT_WIDTH), BF16),
        scratch_shapes=[buf] * (2 * len(ATT_CONFIGS)),
        compiler_params=_params(2),
        name="attn_core",
    )(*ins, bias)


def _attention_mixer(x3, g, w_qkv):
    bsz, s, d = x3.shape
    for window, dil in ATT_CONFIGS:
        assert window // dil == ATT_BLOCK and s % (dil * ATT_BLOCK) == 0
    return _attention(_qkv(x3, g, w_qkv)).reshape(bsz * s, ATT_WIDTH)


def _layer(x, seq_shape, i, p, final_norm):
    bsz, s = seq_shape
    d = x.shape[1]
    m, j = i % 4, i // 4
    x3 = x.reshape(bsz, s, d)
    pre, y, w_pre = None, None, None
    if m == 0:
        y = _s5_mixer(x3, p["norm_mix"][i], p["ssm_a_re"][j], p["ssm_a_im"][j], p["ssm_b_re"][j],
                      p["ssm_b_im"][j], p["ssm_c_re"][j], p["ssm_c_im"][j], p["ssm_d"][j], p["ssm_log_dt"][j])
        pre, w_pre = "glu", p["ssm_w_glu"][j]
    elif m == 1:
        x = _conv_mixer(x3, p["norm_mix"][i], p["conv_w_pw1"][j], p["conv_b_pw1"][j], p["conv_w_dw"][j],
                        p["conv_b_dw"][j], p["conv_ln_g"][j], p["conv_ln_b"][j], p["conv_w_pw2"][j],
                        p["conv_b_pw2"][j]).reshape(bsz * s, d)
    elif m == 2:
        x = _gmlp_mixer(x, p["norm_mix"][i], p["gmlp_w_in"][j], p["gmlp_ln_g"][j], p["gmlp_ln_b"][j],
                        p["gmlp_w_s"][j], p["gmlp_b_s"][j], p["gmlp_w_out"][j])
    else:
        y = _attention_mixer(x3, p["norm_mix"][i], p["attn_w_qkv"][j])
        pre, w_pre = "proj", p["attn_w_o"][j]
    return _mlp(x, p["norm_mlp"][i], p["mlp_w_in"][i], p["mlp_w_out"][i], p["norm_final"],
                final_norm=final_norm, pre=pre, y2=y, w_pre=w_pre)


def kernel(x, norm_mix, norm_mlp, norm_final, ssm_a_re, ssm_a_im, ssm_b_re, ssm_b_im, ssm_c_re, ssm_c_im, ssm_d, ssm_log_dt, ssm_w_glu, conv_w_pw1, conv_b_pw1, conv_w_dw, conv_b_dw, conv_ln_g, conv_ln_b, conv_w_pw2, conv_b_pw2, gmlp_w_in, gmlp_ln_g, gmlp_ln_b, gmlp_w_s, gmlp_b_s, gmlp_w_out, attn_w_qkv, attn_w_o, mlp_w_in, mlp_w_out):
    p = dict(locals())
    bsz, s, d = x.shape
    depth = norm_mix.shape[0]
    x = x.reshape(bsz * s, d)
    for i in range(depth):
        x = _layer(x, (bsz, s), i, p, final_norm=(i == depth - 1))
    return x.reshape(bsz, s, d)
```

```python
import functools

import jax
import jax.numpy as jnp
from jax import lax
from jax.experimental import pallas as pl
from jax.experimental.pallas import tpu as pltpu

F32 = jnp.float32
BF16 = jnp.bfloat16

EPS = 1e-6
LANES = 128
SUBLANES = 8
VMEM_LIMIT = 56 * 1024 * 1024

SSM_GROUP = 16
SSM_STATE = 64
SSM_CHUNK = 8
SSM_BATCH = 2
SSM_PITCH = 520
GROUPS_PER_SLAB = LANES // SSM_GROUP
CONV_WIDTH = 31
CONV_HALO = 32
CONV_ROWS = 128
CONV_SPLIT = 1
CONV_PITCH = 2
GMLP_CHUNK = 128
GMLP_HEADS = 4
GMLP_SPLIT = 2
ATT_CONFIGS = ((128, 1), (512, 4), (2048, 16))
ATT_HEADS = 8
HEAD_DIM = 64
ATT_WIDTH = ATT_HEADS * HEAD_DIM
ATT_BLOCK = 128
ATT_PAIRS = ATT_WIDTH // LANES
ATT_UNROLL = 32


def _params(n_axes):
    return pltpu.CompilerParams(dimension_semantics=("arbitrary",) * n_axes,
                                vmem_limit_bytes=VMEM_LIMIT)


def _resident(shape):
    zeros = (0,) * len(shape)
    return pl.BlockSpec(shape, lambda *_: zeros, pipeline_mode=pl.Buffered(1))


def _rms(x, g):
    ms = jnp.mean(x * x, axis=-1, keepdims=True)
    return x * lax.rsqrt(ms + EPS) * g


def _layernorm(x, g, b):
    mu = jnp.mean(x, axis=-1, keepdims=True)
    xc = x - mu
    var = jnp.mean(xc * xc, axis=-1, keepdims=True)
    return xc * lax.rsqrt(var + EPS) * g + b


def _dot(a, b):
    return jnp.dot(a, b, preferred_element_type=F32)


def _mlp_kernel(*refs, pre, final_norm):
    if pre is None:
        x_ref, g_ref, win_ref, wout_ref, gf_ref, o_ref = refs
        x = x_ref[...]
    else:
        x_ref, y_ref, wpre_ref, g_ref, win_ref, wout_ref, gf_ref, o_ref = refs
        d = x_ref.shape[1]
        z = _dot(y_ref[...].astype(BF16), wpre_ref[...])
        x = x_ref[...] + (z[:, :d] * jax.nn.sigmoid(z[:, d:]) if pre == "glu" else z)
    h = _rms(x, g_ref[...]).astype(BF16)
    a = jnp.maximum(_dot(h, win_ref[...]), 0.0)
    a = (a * a).astype(BF16)
    y = x + _dot(a, wout_ref[...])
    if final_norm:
        y = _rms(y, gf_ref[...])
    o_ref[...] = y


def _mlp(x2, g, w_in, w_out, g_final, final_norm, pre=None, y2=None, w_pre=None, tm=512):
    t, d = x2.shape
    dff = w_in.shape[1]
    row = pl.BlockSpec((tm, d), lambda i: (i, 0))
    ins, specs = [x2], [row]
    if pre is not None:
        ins += [y2, w_pre.astype(BF16)]
        specs += [pl.BlockSpec((tm, y2.shape[1]), lambda i: (i, 0)), _resident(w_pre.shape)]
    ins += [g.reshape(1, d), w_in.astype(BF16), w_out.astype(BF16), g_final.reshape(1, d)]
    specs += [_resident((1, d)), _resident((d, dff)), _resident((dff, d)), _resident((1, d))]
    return pl.pallas_call(
        functools.partial(_mlp_kernel, pre=pre, final_norm=final_norm),
        grid=(t // tm,),
        in_specs=specs,
        out_specs=row,
        out_shape=jax.ShapeDtypeStruct((t, d), F32),
        compiler_params=_params(1),
        name="mlp" + ("_" + pre if pre else "") + ("_final" if final_norm else ""),
    )(*ins)


def _rowscale_kernel(x_ref, o_ref):
    x = x_ref[...]
    r = lax.rsqrt(jnp.mean(x * x, axis=-1, keepdims=True) + EPS)
    o_ref[...] = jnp.broadcast_to(r, o_ref.shape)


def _rowscale(x2, tm=1024):
    t, d = x2.shape
    return pl.pallas_call(
        _rowscale_kernel, grid=(t // tm,), in_specs=[pl.BlockSpec((tm, d), lambda i: (i, 0))],
        out_specs=pl.BlockSpec((tm, LANES), lambda i: (i, 0)),
        out_shape=jax.ShapeDtypeStruct((t, LANES), F32), compiler_params=_params(1), name="rms_rowscale",
    )(x2)


def _s5_weights(a_re, a_im, b_re, b_im, c_re, c_im, log_dt):
    hp = lax.Precision.HIGHEST
    n_groups, n_state = a_re.shape
    L, gs = SSM_CHUNK, GROUPS_PER_SLAB
    n_slabs = n_groups // gs
    a = lax.complex(a_re, a_im)
    lam = a * jnp.exp(log_dt)[:, None]
    a_bar = jnp.exp(lam)
    b_bar = ((a_bar - 1.0) / a)[..., None] * lax.complex(b_re, b_im)
    c = lax.complex(c_re, c_im)
    steps = jnp.arange(L + 1, dtype=F32)
    pw = jnp.exp(lam[None] * steps[:, None, None])
    pw_rev = jnp.exp(lam[None] * (L - 1 - steps[:L])[:, None, None])

    kern = jnp.real(jnp.einsum('gpn,lgnq->lgpq', c, pw[:L, :, :, None] * b_bar[None], precision=hp))
    kq = kern.transpose(0, 1, 3, 2).reshape(L, n_slabs, gs, SSM_GROUP, SSM_GROUP)
    toe = jnp.einsum('lcgqp,gh->clgqhp', kq, jnp.eye(gs, dtype=F32), precision=hp)
    toe = toe.reshape(n_slabs, L, LANES, LANES)

    inj = pw_rev[:, :, :, None] * b_bar[None]
    inj = jnp.stack([jnp.real(inj), jnp.imag(inj)], axis=0)
    inj = inj.reshape(2, L, n_slabs, gs, n_state, SSM_GROUP).transpose(2, 0, 1, 3, 5, 4)
    inj = inj.reshape(n_slabs, 2, L, LANES, n_state)
    inj = jnp.concatenate([inj, inj], axis=-1)

    ro = c[None] * pw[1:, :, None, :]
    ro = jnp.stack([jnp.real(ro), -jnp.imag(ro)], axis=0)
    ro = ro.reshape(2, L, n_slabs, gs, SSM_GROUP, n_state).transpose(2, 0, 1, 5, 3, 4)
    ro = ro.reshape(n_slabs, 2, L, n_state, LANES)
    ro = jnp.concatenate([ro, ro], axis=-2)

    a_chunk = pw[L].reshape(n_slabs, gs * n_state // LANES, LANES)
    a1 = jnp.concatenate([jnp.real(a_chunk), jnp.real(a_chunk)], axis=1)
    a2 = jnp.concatenate([-jnp.imag(a_chunk), jnp.imag(a_chunk)], axis=1)
    return toe.astype(BF16), inj.astype(BF16), ro.astype(BF16), a1, a2


def _s5_expand(toe_ref, inj_ref, ro_ref, toe_s, inj_s, ro_s):
    L = SSM_CHUNK
    half = inj_s.shape[1] // 2
    row = lax.broadcasted_iota(jnp.int32, (LANES, LANES), 0)
    lane = lax.broadcasted_iota(jnp.int32, (LANES, LANES), 1)
    zero = jnp.zeros((LANES, LANES), F32)
    blk = lambda k: pl.ds(k * LANES, LANES)
    for j in range(L):
        for i in range(j - (j % 2), L):
            toe_s[blk(j), blk(i)] = toe_ref[i - j] if i >= j else zero.astype(BF16)
    for ri in range(2):
        for gp in range(half // LANES):
            in_pair = (row // SSM_GROUP // 2 == gp) & (row // SSM_GROUP % 2 == lane // SSM_STATE)
            out_pair = lane // SSM_GROUP == 2 * gp + row // SSM_STATE
            for t in range(L):
                inj_s[blk(t), pl.ds(ri * half + gp * LANES, LANES)] = jnp.where(
                    in_pair, inj_ref[ri, t].astype(F32), zero).astype(BF16)
                ro_s[pl.ds(ri * half + gp * LANES, LANES), blk(t)] = jnp.where(
                    out_pair, ro_ref[ri, t].astype(F32), zero).astype(BF16)


def _s5_kernel(x_ref, rs_ref, g_ref, toe_ref, inj_ref, ro_ref, a1_ref, a2_ref, d_ref, y_ref,
               toe_s, inj_s, ro_s, z_s, x_s):
    L = SSM_CHUNK
    nbt = x_ref.shape[0]
    n_chunks = x_ref.shape[1] // L
    n_blk = inj_s.shape[1] // LANES

    @pl.when(pl.program_id(1) == 0)
    def _():
        _s5_expand(toe_ref, inj_ref, ro_ref, toe_s, inj_s, ro_s)

    gain = g_ref[...]

    def tok(b, j):
        rows = pl.ds(j, n_chunks, stride=L)
        return x_ref[b, rows, :] * rs_ref[b, rows, :] * gain

    a_tok = [jnp.concatenate([tok(b, j) for b in range(nbt)], axis=0) for j in range(L)]
    acat = jnp.concatenate([t.astype(BF16) for t in a_tok], axis=1)
    z = _dot(acat, inj_s[...])
    for b in range(nbt):
        for k in range(n_blk):
            z_s[b, pl.ds(k * SSM_PITCH, n_chunks), :] = z[b * n_chunks:(b + 1) * n_chunks, k * LANES:(k + 1) * LANES]

    a1, a2 = a1_ref[...], a2_ref[...]

    def step(c, carry):
        rows = pl.ds(c, n_blk, stride=SSM_PITCH)
        out = []
        for b, (x, xw) in enumerate(carry):
            x_s[b, rows, :] = x
            z = z_s[b, rows, :]
            out.append((a1 * x + a2 * xw + z, a1 * xw - a2 * x + pltpu.roll(z, n_blk // 2, axis=0)))
        return tuple(out)

    zero = jnp.zeros((n_blk, LANES), F32)
    lax.fori_loop(0, n_chunks, step, ((zero, zero),) * nbt, unroll=SUBLANES)
    x_in = jnp.concatenate(
        [jnp.concatenate([x_s[b, pl.ds(k * SSM_PITCH, n_chunks), :] for k in range(n_blk)], axis=1)
         for b in range(nbt)], axis=0).astype(BF16)

    d = d_ref[...]
    pair = 2 * LANES
    for it in range(L // 2):
        k_hi = (it + 1) * pair
        cols = slice(it * pair, (it + 1) * pair)
        o = _dot(acat[:, :k_hi], toe_s[:k_hi, cols]) + _dot(x_in, ro_s[:, cols])
        for s in range(2):
            i = 2 * it + s
            y = jax.nn.gelu(o[:, s * LANES:(s + 1) * LANES] + d * a_tok[i])
            for b in range(nbt):
                y_ref[b, pl.ds(i, n_chunks, stride=L), :] = y[b * n_chunks:(b + 1) * n_chunks]


def _s5_scan(x3, rs3, g, weights, d_skip):
    bsz, s, d = x3.shape
    toe, inj, ro, a1, a2 = weights
    n_slabs = d // LANES
    n_blk = a1.shape[1]
    cw = SSM_CHUNK * LANES
    assert bsz % SSM_BATCH == 0 and s // SSM_CHUNK <= SSM_PITCH

    def slab(shape):
        nd = len(shape)
        return pl.BlockSpec((None,) + tuple(shape), lambda c, b: (c,) + (0,) * nd)

    act = pl.BlockSpec((SSM_BATCH, s, LANES), lambda c, b: (b, 0, c))
    scan_buf = pltpu.VMEM((SSM_BATCH, n_blk * SSM_PITCH, LANES), F32)
    return pl.pallas_call(
        _s5_kernel,
        grid=(n_slabs, bsz // SSM_BATCH),
        in_specs=[act, pl.BlockSpec((SSM_BATCH, s, LANES), lambda c, b: (b, 0, 0)), slab((1, LANES)),
                  slab(toe.shape[1:]), slab(inj.shape[1:]), slab(ro.shape[1:]),
                  slab(a1.shape[1:]), slab(a2.shape[1:]), slab((1, LANES))],
        out_specs=act,
        out_shape=jax.ShapeDtypeStruct((bsz, s, d), F32),
        scratch_shapes=[pltpu.VMEM((cw, cw), BF16), pltpu.VMEM((cw, n_blk * LANES), BF16),
                        pltpu.VMEM((n_blk * LANES, cw), BF16), scan_buf, scan_buf],
        compiler_params=_params(2),
        name="s5_scan",
    )(x3, rs3, g.reshape(n_slabs, 1, LANES), toe, inj, ro, a1, a2, d_skip.reshape(n_slabs, 1, LANES))


def _s5_mixer(x3, g, a_re, a_im, b_re, b_im, c_re, c_im, d_skip, log_dt):
    bsz, s, d = x3.shape
    rs = _rowscale(x3.reshape(bsz * s, d)).reshape(bsz, s, LANES)
    y = _s5_scan(x3, rs, g, _s5_weights(a_re, a_im, b_re, b_im, c_re, c_im, log_dt), d_skip)
    return y.reshape(bsz * s, d)


def _conv_kernel(x_ref, g_ref, w1_ref, b1_ref, wdw_ref, bdw_ref, lng_ref, lnb_ref, w2_ref, b2_ref,
                 o_ref, hist, conv_s):
    ts, d = x_ref.shape
    t = pl.program_id(1)
    n_slabs = d // LANES

    def hrows(r0, n):
        return pl.ds(CONV_PITCH * r0, n, stride=CONV_PITCH)

    @pl.when(t == 0)
    def _():
        for c in range(n_slabs):
            hist[c, hrows(0, CONV_HALO), :] = jnp.zeros((CONV_HALO, LANES), F32)

    @pl.when(t > 0)
    def _():
        for c in range(n_slabs):
            hist[c, hrows(0, CONV_HALO), :] = hist[c, hrows(ts, CONV_HALO), :]

    first = CONV_HALO - (CONV_WIDTH - 1)
    rows = ts // CONV_SPLIT
    xs = [x_ref[pl.ds(sec * rows, rows), :] for sec in range(CONV_SPLIT)]
    zs = [_dot(_rms(x, g_ref[...]).astype(BF16), w1_ref[...]) + b1_ref[...] for x in xs]
    for sec in range(CONV_SPLIT):
        r0 = sec * rows
        gated = zs[sec][:, :d] * jax.nn.sigmoid(zs[sec][:, d:])
        for c in range(n_slabs):
            hist[c, hrows(CONV_HALO + r0, rows), :] = gated[:, c * LANES:(c + 1) * LANES]
        for c in range(n_slabs):
            lanes = pl.ds(c * LANES, LANES)
            for rb in range(rows // CONV_ROWS):
                ro = r0 + rb * CONV_ROWS
                acc = jnp.zeros((CONV_ROWS, LANES), F32)
                for k in range(CONV_WIDTH):
                    acc = acc + wdw_ref[pl.ds(k, 1), lanes] * hist[c, hrows(ro + first + k, CONV_ROWS), :]
                conv_s[pl.ds(ro, CONV_ROWS), lanes] = acc
        y = _layernorm(conv_s[pl.ds(r0, rows), :] + bdw_ref[...], lng_ref[...], lnb_ref[...])
        y = (y * jax.nn.sigmoid(y)).astype(BF16)
        o_ref[pl.ds(r0, rows), :] = xs[sec] + _dot(y, w2_ref[...]) + b2_ref[...]


def _conv_mixer(x3, g, w_pw1, b_pw1, w_dw, b_dw, ln_g, ln_b, w_pw2, b_pw2, ts=512):
    bsz, s, d = x3.shape
    act = pl.BlockSpec((None, ts, d), lambda b, t: (b, t, 0))
    vec = lambda v: v.reshape(1, -1)
    return pl.pallas_call(
        _conv_kernel,
        grid=(bsz, s // ts),
        in_specs=[act, _resident((1, d)), _resident((d, 2 * d)), _resident((1, 2 * d)),
                  _resident((CONV_WIDTH, d)), _resident((1, d)), _resident((1, d)), _resident((1, d)),
                  _resident((d, d)), _resident((1, d))],
        out_specs=act,
        out_shape=jax.ShapeDtypeStruct((bsz, s, d), F32),
        scratch_shapes=[pltpu.VMEM((d // LANES, CONV_PITCH * (ts + CONV_HALO), LANES), F32),
                        pltpu.VMEM((ts, d), F32)],
        compiler_params=_params(2),
        name="conv_mixer",
    )(x3, vec(g), w_pw1.astype(BF16), vec(b_pw1), w_dw, vec(b_dw), vec(ln_g), vec(ln_b),
      w_pw2.astype(BF16), vec(b_pw2))


def _gmlp_kernel(x_ref, g_ref, win_ref, lng_ref, lnb_ref, ws_ref, bs_ref, wout_ref, o_ref):
    tm, d = x_ref.shape
    e = wout_ref.shape[0]
    he = e // GMLP_HEADS
    rows = lax.broadcasted_iota(jnp.int32, (GMLP_CHUNK, GMLP_CHUNK), 0)
    cols = lax.broadcasted_iota(jnp.int32, (GMLP_CHUNK, GMLP_CHUNK), 1)
    ws = [jnp.where(rows >= cols, ws_ref[h], 0.0).astype(BF16) for h in range(GMLP_HEADS)]
    bs = [bs_ref[:, h:h + 1] for h in range(GMLP_HEADS)]
    sec = tm // GMLP_SPLIT
    xs = [x_ref[pl.ds(i * sec, sec), :] for i in range(GMLP_SPLIT)]
    zs = [_dot(_rms(x, g_ref[...]).astype(BF16), win_ref[...]) for x in xs]
    for i in range(GMLP_SPLIT):
        z = jax.nn.gelu(zs[i])
        u = z[:, :e]
        v = _layernorm(z[:, e:], lng_ref[...], lnb_ref[...]).astype(BF16)
        mixed = []
        for ck in range(sec // GMLP_CHUNK):
            r0 = ck * GMLP_CHUNK
            mixed.append(jnp.concatenate(
                [_dot(ws[h], v[r0:r0 + GMLP_CHUNK, h * he:(h + 1) * he]) + bs[h] for h in range(GMLP_HEADS)],
                axis=1))
        gated = (u * jnp.concatenate(mixed, axis=0)).astype(BF16)
        o_ref[pl.ds(i * sec, sec), :] = xs[i] + _dot(gated, wout_ref[...])


def _gmlp_mixer(x2, g, w_in, ln_g, ln_b, w_s, b_s, w_out, tm=512):
    t, d = x2.shape
    e = w_out.shape[0]
    row = pl.BlockSpec((tm, d), lambda i: (i, 0))
    return pl.pallas_call(
        _gmlp_kernel,
        grid=(t // tm,),
        in_specs=[row, _resident((1, d)), _resident((d, 2 * e)), _resident((1, e)), _resident((1, e)),
                  _resident(w_s.shape), _resident((GMLP_CHUNK, GMLP_HEADS)), _resident((e, d))],
        out_specs=row,
        out_shape=jax.ShapeDtypeStruct((t, d), F32),
        compiler_params=_params(1),
        name="gmlp_mixer",
    )(x2, g.reshape(1, d), w_in.astype(BF16), ln_g.reshape(1, e), ln_b.reshape(1, e), w_s, b_s.T,
      w_out.astype(BF16))


def _qkv_kernel(x_ref, g_ref, w_ref, *refs):
    outs, xs = refs[:-1], refs[-1]
    tm, d = x_ref.shape
    xn = _rms(x_ref[...], g_ref[...])
    gw = 3 * ATT_WIDTH

    def project(h, gi):
        z = _dot(h.astype(BF16), w_ref[:, gi * gw:(gi + 1) * gw])
        return jnp.concatenate([z[:, :ATT_WIDTH] * (HEAD_DIM ** -0.5), z[:, ATT_WIDTH:]], axis=1).astype(BF16)

    for c in range(d // LANES):
        xs[c] = xn[:, c * LANES:(c + 1) * LANES]
    for gi, (_, dil) in sorted(enumerate(ATT_CONFIGS), key=lambda e: e[1][1]):
        if dil == 1:
            outs[gi][0] = project(xn, gi)
            continue
        sub = tm // dil
        h = jnp.concatenate([jnp.concatenate([xs[c, pl.ds(r, sub, stride=dil), :] for c in range(d // LANES)],
                                             axis=1) for r in range(dil)], axis=0)
        z = project(h, gi)
        for r in range(dil):
            outs[gi][r] = z[r * sub:(r + 1) * sub]


def _qkv(x3, g, w_qkv, tm=512):
    bsz, s, d = x3.shape
    n_groups = len(ATT_CONFIGS)
    gw = 3 * ATT_WIDTH
    w = w_qkv.astype(BF16).reshape(d, 3, n_groups, ATT_WIDTH).transpose(0, 2, 1, 3).reshape(d, n_groups * gw)
    outs = [jax.ShapeDtypeStruct((bsz, dil, s // dil, gw), BF16) for _, dil in ATT_CONFIGS]
    ospecs = [pl.BlockSpec((None, dil, tm // dil, gw), lambda b, i: (b, 0, i, 0)) for _, dil in ATT_CONFIGS]
    return pl.pallas_call(
        _qkv_kernel,
        grid=(bsz, s // tm),
        in_specs=[pl.BlockSpec((None, tm, d), lambda b, i: (b, i, 0)), _resident((1, d)), _resident(w.shape)],
        out_specs=ospecs,
        out_shape=outs,
        scratch_shapes=[pltpu.VMEM((d // LANES, tm, LANES), F32)],
        compiler_params=_params(2),
        name="attn_qkv",
    )(x3, g.reshape(1, d), w)


def _att_group(q_ref, k_ref, v_ref, bias_ref, og, lg, dil):
    blk = ATT_BLOCK
    s = q_ref.shape[0]
    nb = s // (dil * blk)

    low = lax.broadcasted_iota(jnp.int32, (1, LANES), 1) < HEAD_DIM

    def body(idx, carry):
        r, n = idx // nb, idx % nb
        cur = pl.ds(pl.multiple_of(idx * blk, blk), blk)
        prev = pl.ds(pl.multiple_of((idx - jnp.minimum(n, 1)) * blk, blk), blk)
        out_rows = pl.ds(r + n * (blk * dil), blk, stride=dil) if dil > 1 else cur
        q = q_ref[cur, :]
        k = jnp.concatenate([k_ref[prev, :], k_ref[cur, :]], axis=0)
        v = jnp.concatenate([v_ref[prev, :], v_ref[cur, :]], axis=0)
        bias = bias_ref[jnp.minimum(n, 1)]
        qq = jnp.concatenate([jnp.where(low, q, jnp.zeros_like(q)), jnp.where(low, jnp.zeros_like(q), q)], axis=0)
        sc = lax.dot_general(qq, k, (((1,), (1,)), ((), ())), preferred_element_type=F32) + bias
        m = jnp.max(sc, axis=-1, keepdims=True)
        p = jnp.exp(sc - m).astype(BF16)
        pv = _dot(p, jnp.concatenate([v, jnp.ones_like(v)], axis=1))
        num = jnp.where(low, pv[:blk, :LANES], pv[blk:, :LANES])
        den = jnp.where(low, pv[:blk, LANES:], pv[blk:, LANES:])
        og[out_rows, :] = num / den
        lg[out_rows, :] = jnp.where(low, m[:blk], m[blk:]) + jnp.log(den)
        return carry

    lax.fori_loop(0, dil * nb, body, 0, unroll=ATT_UNROLL)


def _att_kernel(*refs):
    n_groups = len(ATT_CONFIGS)
    qkv, bias_ref, o_ref = refs[:3 * n_groups], refs[3 * n_groups], refs[3 * n_groups + 1]
    ogs = refs[3 * n_groups + 2:4 * n_groups + 2]
    lgs = refs[4 * n_groups + 2:]
    for gi, (_, dil) in enumerate(ATT_CONFIGS):
        _att_group(qkv[3 * gi], qkv[3 * gi + 1], qkv[3 * gi + 2], bias_ref, ogs[gi], lgs[gi], dil)

    tile = 512

    def merge(t, carry):
        rs = pl.ds(pl.multiple_of(t * tile, tile), tile)
        ls = [lg[rs, :] for lg in lgs]
        m = functools.reduce(jnp.maximum, ls)
        es = [jnp.exp(l - m) for l in ls]
        merged = sum(e * og[rs, :] for e, og in zip(es, ogs)) / sum(es)
        o_ref[rs, :] = merged.astype(BF16)
        return carry

    lax.fori_loop(0, o_ref.shape[0] // tile, merge, 0)


def _attention(qkv_groups):
    bsz, _, _, gw = qkv_groups[0].shape
    s = qkv_groups[0].shape[1] * qkv_groups[0].shape[2]
    per = ATT_WIDTH // LANES
    ins, specs = [], []
    for a in qkv_groups:
        for which in range(3):
            ins.append(a.reshape(bsz, s, gw))
            specs.append(pl.BlockSpec((None, s, LANES), lambda b, p, which=which: (b, 0, which * per + p)))
    i_idx = lax.broadcasted_iota(jnp.int32, (ATT_BLOCK, 2 * ATT_BLOCK), 0)
    j_idx = lax.broadcasted_iota(jnp.int32, (ATT_BLOCK, 2 * ATT_BLOCK), 1)
    dist = i_idx + ATT_BLOCK - j_idx
    band = (dist >= 0) & (dist <= ATT_BLOCK)
    bias = jnp.where(jnp.stack([band & (j_idx >= ATT_BLOCK), band]), 0.0, -jnp.inf).astype(F32)
    bias = jnp.concatenate([bias, bias], axis=1)
    buf = pltpu.VMEM((s, LANES), F32)
    return pl.pallas_call(
        _att_kernel,
        grid=(bsz, ATT_PAIRS),
        in_specs=specs + [_resident(bias.shape)],
        out_specs=pl.BlockSpec((None, s, LANES), lambda b, p: (b, 0, p)),
        out_shape=jax.ShapeDtypeStruct((bsz, s, ATT_WIDTH), BF16),
        scratch_shapes=[buf] * (2 * len(ATT_CONFIGS)),
        compiler_params=_params(2),
        name="attn_core",
    )(*ins, bias)


def _attention_mixer(x3, g, w_qkv):
    bsz, s, d = x3.shape
    for window, dil in ATT_CONFIGS:
        assert window // dil == ATT_BLOCK and s % (dil * ATT_BLOCK) == 0
    return _attention(_qkv(x3, g, w_qkv)).reshape(bsz * s, ATT_WIDTH)


def _layer(x, seq_shape, i, p, final_norm):
    bsz, s = seq_shape
    d = x.shape[1]
    m, j = i % 4, i // 4
    x3 = x.reshape(bsz, s, d)
    pre, y, w_pre = None, None, None
    if m == 0:
        y = _s5_mixer(x3, p["norm_mix"][i], p["ssm_a_re"][j], p["ssm_a_im"][j], p["ssm_b_re"][j],
                      p["ssm_b_im"][j], p["ssm_c_re"][j], p["ssm_c_im"][j], p["ssm_d"][j], p["ssm_log_dt"][j])
        pre, w_pre = "glu", p["ssm_w_glu"][j]
    elif m == 1:
        x = _conv_mixer(x3, p["norm_mix"][i], p["conv_w_pw1"][j], p["conv_b_pw1"][j], p["conv_w_dw"][j],
                        p["conv_b_dw"][j], p["conv_ln_g"][j], p["conv_ln_b"][j], p["conv_w_pw2"][j],
                        p["conv_b_pw2"][j]).reshape(bsz * s, d)
    elif m == 2:
        x = _gmlp_mixer(x, p["norm_mix"][i], p["gmlp_w_in"][j], p["gmlp_ln_g"][j], p["gmlp_ln_b"][j],
                        p["gmlp_w_s"][j], p["gmlp_b_s"][j], p["gmlp_w_out"][j])
    else:
        y = _attention_mixer(x3, p["norm_mix"][i], p["attn_w_qkv"][j])
        pre, w_pre = "proj", p["attn_w_o"][j]
    return _mlp(x, p["norm_mlp"][i], p["mlp_w_in"][i], p["mlp_w_out"][i], p["norm_final"],
                final_norm=final_norm, pre=pre, y2=y, w_pre=w_pre)


def kernel(x, norm_mix, norm_mlp, norm_final, ssm_a_re, ssm_a_im, ssm_b_re, ssm_b_im, ssm_c_re, ssm_c_im, ssm_d, ssm_log_dt, ssm_w_glu, conv_w_pw1, conv_b_pw1, conv_w_dw, conv_b_dw, conv_ln_g, conv_ln_b, conv_w_pw2, conv_b_pw2, gmlp_w_in, gmlp_ln_g, gmlp_ln_b, gmlp_w_s, gmlp_b_s, gmlp_w_out, attn_w_qkv, attn_w_o, mlp_w_in, mlp_w_out):
    p = dict(locals())
    bsz, s, d = x.shape
    depth = norm_mix.shape[0]
    x = x.reshape(bsz * s, d)
    for i in range(depth):
        x = _layer(x, (bsz, s), i, p, final_norm=(i == depth - 1))
    return x.reshape(bsz, s, d)
```
